```python
import jax, jax.numpy as jnp
from jax import lax
import numpy as np

D_MODEL = 2048
BATCH = 4
SEQ = 2048
DEPTH = 1

N_META = 16
POOL_WIDTH = D_MODEL
POOL_WINDOWS = (2, 4, 8, 16)
N_POOL_GROUPS = len(POOL_WINDOWS)
POOL_GROUP = POOL_WIDTH // N_POOL_GROUPS
CONV_WIDTH = D_MODEL
CONV_K = 3
N_BRANCHES = 2
FFN_HIDDEN = ((8 * D_MODEL // 3 + 255) // 256) * 256
IN_PROJ_WIDTH = POOL_WIDTH + 3 * CONV_WIDTH + N_BRANCHES * D_MODEL
EPS = 1e-6

kernel_name = "hybrid_pool_shortconv_gated_block"


def rms_norm(x, g):
    xf = x.astype(jnp.float32)
    y = xf * lax.rsqrt(jnp.mean(xf * xf, axis=-1, keepdims=True) + EPS)
    return (y * g.astype(jnp.float32)).astype(x.dtype)


def causal_multiscale_pool(u):
    b, l, _ = u.shape
    ug = u.reshape(b, l, N_POOL_GROUPS, POOL_GROUP).astype(jnp.float32)
    c = jnp.concatenate([jnp.zeros((b, 1, N_POOL_GROUPS, POOL_GROUP), jnp.float32),
                         jnp.cumsum(ug, axis=1)], axis=1)
    t1 = jnp.arange(1, l + 1, dtype=jnp.float32)
    outs = []
    for gi, w in enumerate(POOL_WINDOWS):
        cg = c[:, :, gi]
        c_lag = jnp.pad(cg, ((0, 0), (w - 1, 0), (0, 0)))[:, :l]
        win_sum = cg[:, 1:] - c_lag
        count = jnp.minimum(t1, jnp.float32(w))[None, :, None]
        outs.append(win_sum / count - ug[:, :, gi])
    return jnp.stack(outs, axis=2).astype(u.dtype)


def causal_depthwise_conv(v, w):
    l = v.shape[1]
    vp = jnp.pad(v, ((0, 0), (CONV_K - 1, 0), (0, 0)))
    return sum(w[k][None, None, :] * vp[:, k:k + l] for k in range(CONV_K))


def setup_inputs(seed: int = 0) -> dict:
    key = jax.random.key(seed)
    ks = jax.random.split(key, 16)
    f32 = jnp.float32
    nrm = lambda k, shape, scale: jax.random.normal(k, shape, f32) * scale
    return {
        "x": nrm(ks[0], (BATCH, SEQ, D_MODEL), 1.0),
        "meta_tokens": nrm(ks[1], (N_META, D_MODEL), 1.0),
        "norm_mix_g": 1.0 + nrm(ks[2], (D_MODEL,), 0.02),
        "w_in": nrm(ks[3], (D_MODEL, IN_PROJ_WIDTH), D_MODEL ** -0.5),
        "b_gate": nrm(ks[4], (N_BRANCHES * D_MODEL,), 0.02),
        "pool_w": nrm(ks[5], (N_POOL_GROUPS, POOL_GROUP, POOL_GROUP), POOL_GROUP ** -0.5),
        "pool_scale": 1.0 + nrm(ks[6], (POOL_WIDTH,), 0.02),
        "conv_w": nrm(ks[7], (CONV_K, CONV_WIDTH), CONV_K ** -0.5),
        "conv_out_w": nrm(ks[8], (CONV_WIDTH, D_MODEL), CONV_WIDTH ** -0.5),
        "w_o": nrm(ks[9], (D_MODEL, D_MODEL), D_MODEL ** -0.5),
        "norm_ffn_g": 1.0 + nrm(ks[10], (D_MODEL,), 0.02),
        "w_gate_up": nrm(ks[11], (D_MODEL, 2 * FFN_HIDDEN), D_MODEL ** -0.5),
        "w_down": nrm(ks[12], (FFN_HIDDEN, D_MODEL), FFN_HIDDEN ** -0.5),
        "norm_final_g": 1.0 + nrm(ks[13], (D_MODEL,), 0.02),
    }


def reference(x, meta_tokens, norm_mix_g, w_in, b_gate, pool_w, pool_scale, conv_w,
              conv_out_w, w_o, norm_ffn_g, w_gate_up, w_down, norm_final_g):
    b = x.shape[0]
    meta = jnp.broadcast_to(meta_tokens[None].astype(x.dtype), (b, N_META, D_MODEL))
    h = jnp.concatenate([meta, x], axis=1)

    splits = np.cumsum([POOL_WIDTH, CONV_WIDTH, CONV_WIDTH, CONV_WIDTH, D_MODEL]).tolist()
    for _ in range(DEPTH):
        hn = rms_norm(h, norm_mix_g)
        proj = hn @ w_in
        u, gb_in, gc_in, v_in, ga_lin, gbr_lin = jnp.split(proj, splits, axis=-1)

        pooled = causal_multiscale_pool(u)
        y_a = jnp.einsum("blgc,gcd->blgd", pooled, pool_w).reshape(b, -1, POOL_WIDTH)
        y_a = y_a * pool_scale

        y_b = (gb_in * causal_depthwise_conv(gc_in * v_in, conv_w)) @ conv_out_w

        gates = jax.nn.sigmoid(jnp.concatenate([ga_lin, gbr_lin], axis=-1) + b_gate)
        g_a, g_b = jnp.split(gates, 2, axis=-1)
        h = h + (g_a * y_a + g_b * y_b) @ w_o

        hn = rms_norm(h, norm_ffn_g)
        gate, up = jnp.split(hn @ w_gate_up, 2, axis=-1)
        h = h + (jax.nn.silu(gate) * up) @ w_down

    out = rms_norm(h, norm_final_g)
    return out[:, N_META:]
```

```python
import functools

import jax
import jax.numpy as jnp
from jax import lax
from jax.experimental import pallas as pl
from jax.experimental.pallas import tpu as pltpu

D_MODEL = 2048
N_META = 16
POOL_WINDOWS = (2, 4, 8, 16)
POOL_GROUP = D_MODEL // len(POOL_WINDOWS)
CONV_K = 3
FFN_HIDDEN = ((8 * D_MODEL // 3 + 255) // 256) * 256
EPS = 1e-6

F32 = jnp.float32
BF16 = jnp.bfloat16

MIB = 1024 * 1024
ROW_TILE = 512
CONV_CHUNK = 256
FFN_CHUNK = 512
MIX_ROWS = 256
DOWN_ROWS = 1024

assert N_META >= max(POOL_WINDOWS) and N_META >= CONV_K


def _dot(a, b):
    return lax.dot_general(a, b, (((1,), (0,)), ((), ())), preferred_element_type=F32)


def _rms(xf, g):
    ms = jnp.mean(xf * xf, axis=-1, keepdims=True)
    return xf * lax.rsqrt(ms + EPS) * g


def _params(vmem_mib, semantics):
    return pltpu.CompilerParams(dimension_semantics=semantics,
                                vmem_limit_bytes=vmem_mib * MIB)


def _norm_in_kernel(x_ref, meta_ref, g_ref, o_ref):
    i = pl.program_id(1)
    g = g_ref[...]

    @pl.when(i == 0)
    def _():
        o_ref[0, 0:N_META, :] = _rms(meta_ref[...], g).astype(BF16)

    r = pl.multiple_of(i * ROW_TILE, ROW_TILE)
    o_ref[0, pl.ds(N_META + r, ROW_TILE), :] = _rms(x_ref[0], g).astype(BF16)


def _norm_in(x, meta, g):
    b, seq, d = x.shape
    return pl.pallas_call(
        _norm_in_kernel,
        grid=(b, seq // ROW_TILE),
        in_specs=[pl.BlockSpec((1, ROW_TILE, d), lambda bi, i: (bi, i, 0)),
                  pl.BlockSpec((N_META, d), lambda bi, i: (0, 0)),
                  pl.BlockSpec((1, d), lambda bi, i: (0, 0))],
        out_specs=pl.BlockSpec((1, seq + N_META, d), lambda bi, i: (bi, 0, 0)),
        out_shape=jax.ShapeDtypeStruct((b, seq + N_META, d), BF16),
        compiler_params=_params(40, ("arbitrary", "arbitrary")),
        name="norm_in",
    )(x, meta, g)


def _pool_kernel(hn_ref, wu_ref, wga_ref, bga_ref, pw_ref, ps_ref, o_ref, wu_s, wga_s, pw_s):
    grp = pl.program_id(1)
    wu_s[...] = wu_ref[...].astype(BF16)
    wga_s[...] = wga_ref[...].astype(BF16)
    pw_s[...] = pw_ref[0].astype(BF16)
    inv_w = jnp.where(grp == 0, 0.5, jnp.where(grp == 1, 0.25, jnp.where(grp == 2, 0.125, 0.0625))).astype(F32)
    n_tiles = (hn_ref.shape[1] - N_META) // ROW_TILE

    def body(i, carry):
        r = pl.multiple_of(i * ROW_TILE, ROW_TILE)
        lhs = hn_ref[0, pl.ds(r, ROW_TILE + N_META), :]
        u = _dot(lhs, wu_s[...])
        s = u + pltpu.roll(u, 1, 0)
        s = lax.cond(grp >= 1, lambda t: t + pltpu.roll(t, 2, 0), lambda t: t, s)
        s = lax.cond(grp >= 2, lambda t: t + pltpu.roll(t, 4, 0), lambda t: t, s)
        s = lax.cond(grp >= 3, lambda t: t + pltpu.roll(t, 8, 0), lambda t: t, s)
        pooled = (s * inv_w - u)[N_META:, :]
        ya = _dot(pooled.astype(BF16), pw_s[...]) * ps_ref[...]
        ga = _dot(lhs[N_META:, :], wga_s[...]) + bga_ref[...]
        o_ref[0, pl.ds(r, ROW_TILE), :] = (jax.nn.sigmoid(ga) * ya).astype(BF16)
        return carry

    lax.fori_loop(0, n_tiles, body, 0)


def _pool_branch(hn, w_in, b_gate, pool_w, pool_scale):
    b, l, d = hn.shape
    seq = l - N_META
    n_grp = len(POOL_WINDOWS)
    pg = POOL_GROUP
    ga_blk = (4 * d) // pg
    return pl.pallas_call(
        _pool_kernel,
        grid=(b, n_grp),
        in_specs=[pl.BlockSpec((1, l, d), lambda bi, g: (bi, 0, 0)),
                  pl.BlockSpec((d, pg), lambda bi, g: (0, g)),
                  pl.BlockSpec((d, pg), lambda bi, g: (0, ga_blk + g)),
                  pl.BlockSpec((1, pg), lambda bi, g: (0, g)),
                  pl.BlockSpec((1, pg, pg), lambda bi, g: (g, 0, 0)),
                  pl.BlockSpec((1, pg), lambda bi, g: (0, g))],
        out_specs=pl.BlockSpec((1, seq, pg), lambda bi, g: (bi, 0, g)),
        out_shape=jax.ShapeDtypeStruct((b, seq, d), BF16),
        scratch_shapes=[pltpu.VMEM((d, pg), BF16), pltpu.VMEM((d, pg), BF16),
                        pltpu.VMEM((pg, pg), BF16)],
        compiler_params=_params(56, ("arbitrary", "arbitrary")),
        name="pool_branch",
    )(hn, w_in, w_in, b_gate, pool_w, pool_scale)


def _conv_kernel(hn_ref, wgb_ref, wgc_ref, wv_ref, wgr_ref, bgr_ref, cw_ref, z_ref, gs_ref,
                 wgb_s, wgc_s, wv_s, wgr_s):
    wgb_s[...] = wgb_ref[...].astype(BF16)
    wgc_s[...] = wgc_ref[...].astype(BF16)
    wv_s[...] = wv_ref[...].astype(BF16)
    wgr_s[...] = wgr_ref[...].astype(BF16)
    w0 = cw_ref[0:1, :]
    w1 = cw_ref[1:2, :]
    w2 = cw_ref[2:3, :]
    n_tiles = (hn_ref.shape[1] - N_META) // ROW_TILE

    def body(i, carry):
        r = pl.multiple_of(i * ROW_TILE, ROW_TILE)
        lhs = hn_ref[0, pl.ds(r, ROW_TILE + N_META), :]
        lhs_out = lhs[N_META:, :]
        cv = _dot(lhs, wgc_s[...]) * _dot(lhs, wv_s[...])
        conv = w0 * pltpu.roll(cv, 2, 0) + w1 * pltpu.roll(cv, 1, 0) + w2 * cv
        z = _dot(lhs_out, wgb_s[...]) * conv[N_META:, :]
        z_ref[0, pl.ds(r, ROW_TILE), :] = z.astype(BF16)
        gr = _dot(lhs_out, wgr_s[...]) + bgr_ref[...]
        gs_ref[0, pl.ds(r, ROW_TILE), :] = jax.nn.sigmoid(gr).astype(BF16)
        return carry

    lax.fori_loop(0, n_tiles, body, 0)


def _conv_branch(hn, w_in, b_gate, conv_w):
    b, l, d = hn.shape
    seq = l - N_META
    cc = CONV_CHUNK
    n_chunks = d // cc
    wspec = lambda first: pl.BlockSpec((d, cc), lambda bi, c: (0, first + c))
    out_spec = pl.BlockSpec((1, seq, cc), lambda bi, c: (bi, 0, c))
    return pl.pallas_call(
        _conv_kernel,
        grid=(b, n_chunks),
        in_specs=[pl.BlockSpec((1, l, d), lambda bi, c: (bi, 0, 0)),
                  wspec(1 * n_chunks), wspec(2 * n_chunks), wspec(3 * n_chunks), wspec(5 * n_chunks),
                  pl.BlockSpec((1, cc), lambda bi, c: (0, n_chunks + c)),
                  pl.BlockSpec((CONV_K, cc), lambda bi, c: (0, c))],
        out_specs=[out_spec, out_spec],
        out_shape=[jax.ShapeDtypeStruct((b, seq, d), BF16)] * 2,
        scratch_shapes=[pltpu.VMEM((d, cc), BF16)] * 4,
        compiler_params=_params(56, ("arbitrary", "arbitrary")),
        name="conv_branch",
    )(hn, w_in, w_in, w_in, w_in, b_gate, conv_w)


def _mix_kernel(z_ref, gaya_ref, gs_ref, x_ref, cw_ref, wo_ref, g_ref, h1_ref, hn2_ref):
    yb = _dot(z_ref[...], cw_ref[...])
    mix = gaya_ref[...].astype(F32) + gs_ref[...].astype(F32) * yb
    h1 = x_ref[...] + _dot(mix.astype(BF16), wo_ref[...])
    h1_ref[...] = h1
    hn2_ref[...] = _rms(h1, g_ref[...]).astype(BF16)


def _mix(z, gaya, gs, x2d, conv_out_w, w_o, g):
    m, d = x2d.shape
    rows = pl.BlockSpec((MIX_ROWS, d), lambda i: (i, 0))
    whole = pl.BlockSpec((d, d), lambda i: (0, 0), pipeline_mode=pl.Buffered(1))
    return pl.pallas_call(
        _mix_kernel,
        grid=(m // MIX_ROWS,),
        in_specs=[rows, rows, rows, rows, whole, whole, pl.BlockSpec((1, d), lambda i: (0, 0))],
        out_specs=[rows, rows],
        out_shape=[jax.ShapeDtypeStruct((m, d), F32), jax.ShapeDtypeStruct((m, d), BF16)],
        compiler_params=_params(56, ("arbitrary",)),
        name="mix_out_proj",
    )(z, gaya, gs, x2d, conv_out_w, w_o, g)


def _ffn_up_kernel(hn_ref, wg_ref, wu_ref, o_ref, wg_s, wu_s):
    wg_s[...] = wg_ref[...].astype(BF16)
    wu_s[...] = wu_ref[...].astype(BF16)
    n_tiles = hn_ref.shape[1] // ROW_TILE

    def body(i, carry):
        r = pl.multiple_of(i * ROW_TILE, ROW_TILE)
        lhs = hn_ref[0, pl.ds(r, ROW_TILE), :]
        gate = _dot(lhs, wg_s[...])
        up = _dot(lhs, wu_s[...])
        o_ref[0, pl.ds(r, ROW_TILE), :] = (jax.nn.silu(gate) * up).astype(BF16)
        return carry

    lax.fori_loop(0, n_tiles, body, 0)


def _ffn_up(hn2, w_gate_up):
    b, seq, d = hn2.shape
    fc = FFN_CHUNK
    n_chunks = FFN_HIDDEN // fc
    return pl.pallas_call(
        _ffn_up_kernel,
        grid=(b, n_chunks),
        in_specs=[pl.BlockSpec((1, seq, d), lambda bi, j: (bi, 0, 0)),
                  pl.BlockSpec((d, fc), lambda bi, j: (0, j)),
                  pl.BlockSpec((d, fc), lambda bi, j: (0, n_chunks + j))],
        out_specs=pl.BlockSpec((1, seq, fc), lambda bi, j: (bi, 0, j)),
        out_shape=jax.ShapeDtypeStruct((b, seq, FFN_HIDDEN), BF16),
        scratch_shapes=[pltpu.VMEM((d, fc), BF16)] * 2,
        compiler_params=_params(56, ("arbitrary", "arbitrary")),
        name="ffn_up",
    )(hn2, w_gate_up, w_gate_up)


def _ffn_down_kernel(act_ref, wd_ref, h1_ref, g_ref, o_ref):
    k = pl.program_id(1)

    @pl.when(k == 0)
    def _():
        o_ref[...] = h1_ref[...]

    o_ref[...] += _dot(act_ref[...], wd_ref[...])

    @pl.when(k == pl.num_programs(1) - 1)
    def _():
        o_ref[...] = _rms(o_ref[...], g_ref[...])


def _ffn_down(act2d, w_down, h1, g):
    m, d = h1.shape
    fc = FFN_CHUNK
    return pl.pallas_call(
        _ffn_down_kernel,
        grid=(m // DOWN_ROWS, FFN_HIDDEN // fc),
        in_specs=[pl.BlockSpec((DOWN_ROWS, fc), lambda i, k: (i, k)),
                  pl.BlockSpec((fc, d), lambda i, k: (k, 0)),
                  pl.BlockSpec((DOWN_ROWS, d), lambda i, k: (i, 0)),
                  pl.BlockSpec((1, d), lambda i, k: (0, 0))],
        out_specs=pl.BlockSpec((DOWN_ROWS, d), lambda i, k: (i, 0)),
        out_shape=jax.ShapeDtypeStruct((m, d), F32),
        compiler_params=_params(56, ("arbitrary", "arbitrary")),
        name="ffn_down",
    )(act2d, w_down, h1, g)


def kernel(x, meta_tokens, norm_mix_g, w_in, b_gate, pool_w, pool_scale, conv_w, conv_out_w, w_o,
           norm_ffn_g, w_gate_up, w_down, norm_final_g):
    b, seq, d = x.shape
    row = lambda v: v.reshape(1, -1)
    hn = _norm_in(x, meta_tokens, row(norm_mix_g))
    gaya = _pool_branch(hn, w_in, row(b_gate), pool_w, row(pool_scale))
    z, gs = _conv_branch(hn, w_in, row(b_gate), conv_w)
    m = b * seq
    h1, hn2 = _mix(z.reshape(m, d), gaya.reshape(m, d), gs.reshape(m, d), x.reshape(m, d),
                   conv_out_w, w_o, row(norm_ffn_g))
    act = _ffn_up(hn2.reshape(b, seq, d), w_gate_up)
    out = _ffn_down(act.reshape(m, FFN_HIDDEN), w_down, h1, row(norm_final_g))
    return out.reshape(b, seq, d)
```

```python
import functools

import jax
import jax.numpy as jnp
from jax import lax
from jax.experimental import pallas as pl
from jax.experimental.pallas import tpu as pltpu

D_MODEL = 2048
N_META = 16
POOL_WINDOWS = (2, 4, 8, 16)
POOL_GROUP = D_MODEL // len(POOL_WINDOWS)
CONV_K = 3
FFN_HIDDEN = ((8 * D_MODEL // 3 + 255) // 256) * 256
EPS = 1e-6

F32 = jnp.float32
BF16 = jnp.bfloat16

MIB = 1024 * 1024
ROW_TILE = 1024
CONV_CHUNK = 256
FFN_CHUNK = 512
MIX_ROWS = 256
DOWN_ROWS = 1024

assert N_META >= max(POOL_WINDOWS) and N_META >= CONV_K


def _dot(a, b):
    return lax.dot_general(a, b, (((1,), (0,)), ((), ())), preferred_element_type=F32)


def _rms(xf, g):
    ms = jnp.mean(xf * xf, axis=-1, keepdims=True)
    return xf * lax.rsqrt(ms + EPS) * g


def _params(vmem_mib, semantics):
    return pltpu.CompilerParams(dimension_semantics=semantics,
                                vmem_limit_bytes=vmem_mib * MIB)


def _norm_in_kernel(x_ref, meta_ref, g_ref, o_ref):
    i = pl.program_id(1)
    g = g_ref[...]

    @pl.when(i == 0)
    def _():
        o_ref[0, 0:N_META, :] = _rms(meta_ref[...], g).astype(BF16)

    r = pl.multiple_of(i * ROW_TILE, ROW_TILE)
    o_ref[0, pl.ds(N_META + r, ROW_TILE), :] = _rms(x_ref[0], g).astype(BF16)


def _norm_in(x, meta, g):
    b, seq, d = x.shape
    return pl.pallas_call(
        _norm_in_kernel,
        grid=(b, seq // ROW_TILE),
        in_specs=[pl.BlockSpec((1, ROW_TILE, d), lambda bi, i: (bi, i, 0)),
                  pl.BlockSpec((N_META, d), lambda bi, i: (0, 0)),
                  pl.BlockSpec((1, d), lambda bi, i: (0, 0))],
        out_specs=pl.BlockSpec((1, seq + N_META, d), lambda bi, i: (bi, 0, 0)),
        out_shape=jax.ShapeDtypeStruct((b, seq + N_META, d), BF16),
        compiler_params=_params(40, ("arbitrary", "arbitrary")),
        name="norm_in",
    )(x, meta, g)


def _pool_kernel(hn_ref, wu_ref, wga_ref, bga_ref, pw_ref, ps_ref, o_ref):
    grp = pl.program_id(1)
    n_tiles = (hn_ref.shape[1] - N_META) // ROW_TILE

    def body(window, i, carry):
        r = pl.multiple_of(i * ROW_TILE, ROW_TILE)
        lhs = hn_ref[0, pl.ds(r, ROW_TILE + N_META), :]
        u = _dot(lhs, wu_ref[...])
        s = u
        shift = 1
        while shift < window:
            s = s + pltpu.roll(s, shift, 0)
            shift *= 2
        pooled = (s * (1.0 / window) - u)[N_META:, :]
        ya = _dot(pooled.astype(BF16), pw_ref[0]) * ps_ref[...]
        ga = _dot(lhs[N_META:, :], wga_ref[...]) + bga_ref[...]
        o_ref[0, pl.ds(r, ROW_TILE), :] = (jax.nn.sigmoid(ga) * ya).astype(BF16)
        return carry

    for k, window in enumerate(POOL_WINDOWS):
        @pl.when(grp == k)
        def _(window=window):
            lax.fori_loop(0, n_tiles, functools.partial(body, window), 0)


def _pool_branch(hn, w_in, b_gate, pool_w, pool_scale):
    b, l, d = hn.shape
    seq = l - N_META
    n_grp = len(POOL_WINDOWS)
    pg = POOL_GROUP
    ga_blk = (4 * d) // pg
    return pl.pallas_call(
        _pool_kernel,
        grid=(b, n_grp),
        in_specs=[pl.BlockSpec((1, l, d), lambda bi, g: (bi, 0, 0)),
                  pl.BlockSpec((d, pg), lambda bi, g: (0, g)),
                  pl.BlockSpec((d, pg), lambda bi, g: (0, ga_blk + g)),
                  pl.BlockSpec((1, pg), lambda bi, g: (0, g)),
                  pl.BlockSpec((1, pg, pg), lambda bi, g: (g, 0, 0)),
                  pl.BlockSpec((1, pg), lambda bi, g: (0, g))],
        out_specs=pl.BlockSpec((1, seq, pg), lambda bi, g: (bi, 0, g)),
        out_shape=jax.ShapeDtypeStruct((b, seq, d), BF16),
        compiler_params=_params(56, ("arbitrary", "arbitrary")),
        name="pool_branch",
    )(hn, w_in, w_in, b_gate, pool_w, pool_scale)


def _conv_kernel(hn_ref, wgb_ref, wgc_ref, wv_ref, wgr_ref, bgr_ref, cw_ref, z_ref, gs_ref):
    w0 = cw_ref[0:1, :]
    w1 = cw_ref[1:2, :]
    w2 = cw_ref[2:3, :]
    n_tiles = (hn_ref.shape[1] - N_META) // ROW_TILE

    def body(i, carry):
        r = pl.multiple_of(i * ROW_TILE, ROW_TILE)
        lhs = hn_ref[0, pl.ds(r, ROW_TILE + N_META), :]
        lhs_out = lhs[N_META:, :]
        cv = _dot(lhs, wgc_ref[...]) * _dot(lhs, wv_ref[...])
        conv = w0 * pltpu.roll(cv, 2, 0) + w1 * pltpu.roll(cv, 1, 0) + w2 * cv
        z = _dot(lhs_out, wgb_ref[...]) * conv[N_META:, :]
        z_ref[0, pl.ds(r, ROW_TILE), :] = z.astype(BF16)
        gr = _dot(lhs_out, wgr_ref[...]) + bgr_ref[...]
        gs_ref[0, pl.ds(r, ROW_TILE), :] = jax.nn.sigmoid(gr).astype(BF16)
        return carry

    lax.fori_loop(0, n_tiles, body, 0)


def _conv_branch(hn, w_in, b_gate, conv_w):
    b, l, d = hn.shape
    seq = l - N_META
    cc = CONV_CHUNK
    n_chunks = d // cc
    wspec = lambda first: pl.BlockSpec((d, cc), lambda bi, c: (0, first + c))
    out_spec = pl.BlockSpec((1, seq, cc), lambda bi, c: (bi, 0, c))
    return pl.pallas_call(
        _conv_kernel,
        grid=(b, n_chunks),
        in_specs=[pl.BlockSpec((1, l, d), lambda bi, c: (bi, 0, 0)),
                  wspec(1 * n_chunks), wspec(2 * n_chunks), wspec(3 * n_chunks), wspec(5 * n_chunks),
                  pl.BlockSpec((1, cc), lambda bi, c: (0, n_chunks + c)),
                  pl.BlockSpec((CONV_K, cc), lambda bi, c: (0, c))],
        out_specs=[out_spec, out_spec],
        out_shape=[jax.ShapeDtypeStruct((b, seq, d), BF16)] * 2,
        compiler_params=_params(56, ("arbitrary", "arbitrary")),
        name="conv_branch",
    )(hn, w_in, w_in, w_in, w_in, b_gate, conv_w)


def _mix_kernel(z_ref, gaya_ref, gs_ref, x_ref, cw_ref, wo_ref, g_ref, h1_ref, hn2_ref):
    yb = _dot(z_ref[...], cw_ref[...])
    mix = gaya_ref[...].astype(F32) + gs_ref[...].astype(F32) * yb
    h1 = x_ref[...] + _dot(mix.astype(BF16), wo_ref[...])
    h1_ref[...] = h1
    hn2_ref[...] = _rms(h1, g_ref[...]).astype(BF16)


def _mix(z, gaya, gs, x2d, conv_out_w, w_o, g):
    m, d = x2d.shape
    rows = pl.BlockSpec((MIX_ROWS, d), lambda i: (i, 0))
    whole = pl.BlockSpec((d, d), lambda i: (0, 0), pipeline_mode=pl.Buffered(1))
    return pl.pallas_call(
        _mix_kernel,
        grid=(m // MIX_ROWS,),
        in_specs=[rows, rows, rows, rows, whole, whole, pl.BlockSpec((1, d), lambda i: (0, 0))],
        out_specs=[rows, rows],
        out_shape=[jax.ShapeDtypeStruct((m, d), F32), jax.ShapeDtypeStruct((m, d), BF16)],
        compiler_params=_params(56, ("arbitrary",)),
        name="mix_out_proj",
    )(z, gaya, gs, x2d, conv_out_w, w_o, g)


def _ffn_up_kernel(hn_ref, wg_ref, wu_ref, o_ref):
    n_tiles = hn_ref.shape[1] // ROW_TILE

    def body(i, carry):
        r = pl.multiple_of(i * ROW_TILE, ROW_TILE)
        lhs = hn_ref[0, pl.ds(r, ROW_TILE), :]
        gate = _dot(lhs, wg_ref[...])
        up = _dot(lhs, wu_ref[...])
        o_ref[0, pl.ds(r, ROW_TILE), :] = (jax.nn.silu(gate) * up).astype(BF16)
        return carry

    lax.fori_loop(0, n_tiles, body, 0)


def _ffn_up(hn2, w_gate_up):
    b, seq, d = hn2.shape
    fc = FFN_CHUNK
    n_chunks = FFN_HIDDEN // fc
    return pl.pallas_call(
        _ffn_up_kernel,
        grid=(b, n_chunks),
        in_specs=[pl.BlockSpec((1, seq, d), lambda bi, j: (bi, 0, 0)),
                  pl.BlockSpec((d, fc), lambda bi, j: (0, j)),
                  pl.BlockSpec((d, fc), lambda bi, j: (0, n_chunks + j))],
        out_specs=pl.BlockSpec((1, seq, fc), lambda bi, j: (bi, 0, j)),
        out_shape=jax.ShapeDtypeStruct((b, seq, FFN_HIDDEN), BF16),
        compiler_params=_params(56, ("arbitrary", "arbitrary")),
        name="ffn_up",
    )(hn2, w_gate_up, w_gate_up)


def _ffn_down_kernel(act_ref, wd_ref, h1_ref, g_ref, o_ref):
    k = pl.program_id(1)

    @pl.when(k == 0)
    def _():
        o_ref[...] = h1_ref[...]

    o_ref[...] += _dot(act_ref[...], wd_ref[...])

    @pl.when(k == pl.num_programs(1) - 1)
    def _():
        o_ref[...] = _rms(o_ref[...], g_ref[...])


def _ffn_down(act2d, w_down, h1, g):
    m, d = h1.shape
    fc = FFN_CHUNK
    return pl.pallas_call(
        _ffn_down_kernel,
        grid=(m // DOWN_ROWS, FFN_HIDDEN // fc),
        in_specs=[pl.BlockSpec((DOWN_ROWS, fc), lambda i, k: (i, k)),
                  pl.BlockSpec((fc, d), lambda i, k: (k, 0)),
                  pl.BlockSpec((DOWN_ROWS, d), lambda i, k: (i, 0)),
                  pl.BlockSpec((1, d), lambda i, k: (0, 0))],
        out_specs=pl.BlockSpec((DOWN_ROWS, d), lambda i, k: (i, 0)),
        out_shape=jax.ShapeDtypeStruct((m, d), F32),
        compiler_params=_params(56, ("arbitrary", "arbitrary")),
        name="ffn_down",
    )(act2d, w_down, h1, g)


def kernel(x, meta_tokens, norm_mix_g, w_in, b_gate, pool_w, pool_scale, conv_w, conv_out_w, w_o,
           norm_ffn_g, w_gate_up, w_down, norm_final_g):
    b, seq, d = x.shape
    row = lambda v: v.reshape(1, -1)
    hn = _norm_in(x, meta_tokens, row(norm_mix_g))
    gaya = _pool_branch(hn, w_in, row(b_gate), pool_w, row(pool_scale))
    z, gs = _conv_branch(hn, w_in, row(b_gate), conv_w)
    m = b * seq
    h1, hn2 = _mix(z.reshape(m, d), gaya.reshape(m, d), gs.reshape(m, d), x.reshape(m, d),
                   conv_out_w, w_o, row(norm_ffn_g))
    act = _ffn_up(hn2.reshape(b, seq, d), w_gate_up)
    out = _ffn_down(act.reshape(m, FFN_HIDDEN), w_down, h1, row(norm_final_g))
    return out.reshape(b, seq, d)
```

```python
import functools

import jax
import jax.numpy as jnp
from jax import lax
from jax.experimental import pallas as pl
from jax.experimental.pallas import tpu as pltpu

D_MODEL = 2048
N_META = 16
POOL_WINDOWS = (2, 4, 8, 16)
POOL_GROUP = D_MODEL // len(POOL_WINDOWS)
CONV_K = 3
FFN_HIDDEN = ((8 * D_MODEL // 3 + 255) // 256) * 256
EPS = 1e-6

F32 = jnp.float32
BF16 = jnp.bfloat16

MIB = 1024 * 1024
ROW_TILE = 1024
CONV_CHUNK = 256
FFN_CHUNK = 512
MIX_ROWS = 256
DOWN_COLS = 512
FINISH_ROWS = 256

assert N_META >= max(POOL_WINDOWS) and N_META >= CONV_K


def _dot(a, b):
    return lax.dot_general(a, b, (((1,), (0,)), ((), ())), preferred_element_type=F32)


def _rms(xf, g):
    ms = jnp.mean(xf * xf, axis=-1, keepdims=True)
    return xf * lax.rsqrt(ms + EPS) * g


def _params(vmem_mib, semantics):
    return pltpu.CompilerParams(dimension_semantics=semantics,
                                vmem_limit_bytes=vmem_mib * MIB)


def _norm_in_kernel(x_ref, meta_ref, g_ref, o_ref):
    i = pl.program_id(1)
    g = g_ref[...]

    @pl.when(i == 0)
    def _():
        o_ref[0, 0:N_META, :] = _rms(meta_ref[...], g).astype(BF16)

    r = pl.multiple_of(i * ROW_TILE, ROW_TILE)
    o_ref[0, pl.ds(N_META + r, ROW_TILE), :] = _rms(x_ref[0], g).astype(BF16)


def _norm_in(x, meta, g):
    b, seq, d = x.shape
    return pl.pallas_call(
        _norm_in_kernel,
        grid=(b, seq // ROW_TILE),
        in_specs=[pl.BlockSpec((1, ROW_TILE, d), lambda bi, i: (bi, i, 0)),
                  pl.BlockSpec((N_META, d), lambda bi, i: (0, 0)),
                  pl.BlockSpec((1, d), lambda bi, i: (0, 0))],
        out_specs=pl.BlockSpec((1, seq + N_META, d), lambda bi, i: (bi, 0, 0)),
        out_shape=jax.ShapeDtypeStruct((b, seq + N_META, d), BF16),
        compiler_params=_params(40, ("arbitrary", "arbitrary")),
        name="norm_in",
    )(x, meta, g)


def _pool_kernel(hn_ref, wu_ref, wga_ref, bga_ref, pw_ref, ps_ref, o_ref):
    grp = pl.program_id(1)
    n_tiles = (hn_ref.shape[1] - N_META) // ROW_TILE

    def body(window, i, carry):
        r = pl.multiple_of(i * ROW_TILE, ROW_TILE)
        lhs = hn_ref[0, pl.ds(r, ROW_TILE + N_META), :]
        u = _dot(lhs, wu_ref[...])
        s = u
        shift = 1
        while shift < window:
            s = s + pltpu.roll(s, shift, 0)
            shift *= 2
        pooled = (s * (1.0 / window) - u)[N_META:, :]
        ya = _dot(pooled.astype(BF16), pw_ref[0]) * ps_ref[...]
        ga = _dot(lhs[N_META:, :], wga_ref[...]) + bga_ref[...]
        o_ref[0, pl.ds(r, ROW_TILE), :] = (jax.nn.sigmoid(ga) * ya).astype(BF16)
        return carry

    for k, window in enumerate(POOL_WINDOWS):
        @pl.when(grp == k)
        def _(window=window):
            lax.fori_loop(0, n_tiles, functools.partial(body, window), 0)


def _pool_branch(hn, w_in, b_gate, pool_w, pool_scale):
    b, l, d = hn.shape
    seq = l - N_META
    n_grp = len(POOL_WINDOWS)
    pg = POOL_GROUP
    ga_blk = (4 * d) // pg
    return pl.pallas_call(
        _pool_kernel,
        grid=(b, n_grp),
        in_specs=[pl.BlockSpec((1, l, d), lambda bi, g: (bi, 0, 0)),
                  pl.BlockSpec((d, pg), lambda bi, g: (0, g)),
                  pl.BlockSpec((d, pg), lambda bi, g: (0, ga_blk + g)),
                  pl.BlockSpec((1, pg), lambda bi, g: (0, g)),
                  pl.BlockSpec((1, pg, pg), lambda bi, g: (g, 0, 0)),
                  pl.BlockSpec((1, pg), lambda bi, g: (0, g))],
        out_specs=pl.BlockSpec((1, seq, pg), lambda bi, g: (bi, 0, g)),
        out_shape=jax.ShapeDtypeStruct((b, seq, d), BF16),
        compiler_params=_params(56, ("arbitrary", "arbitrary")),
        name="pool_branch",
    )(hn, w_in, w_in, b_gate, pool_w, pool_scale)


def _conv_kernel(hn_ref, wgb_ref, wgc_ref, wv_ref, wgr_ref, bgr_ref, cw_ref, z_ref, gs_ref):
    w0 = cw_ref[0:1, :]
    w1 = cw_ref[1:2, :]
    w2 = cw_ref[2:3, :]
    n_tiles = (hn_ref.shape[1] - N_META) // ROW_TILE

    def body(i, carry):
        r = pl.multiple_of(i * ROW_TILE, ROW_TILE)
        lhs = hn_ref[0, pl.ds(r, ROW_TILE + N_META), :]
        lhs_out = lhs[N_META:, :]
        cv = _dot(lhs, wgc_ref[...]) * _dot(lhs, wv_ref[...])
        conv = w0 * pltpu.roll(cv, 2, 0) + w1 * pltpu.roll(cv, 1, 0) + w2 * cv
        z = _dot(lhs_out, wgb_ref[...]) * conv[N_META:, :]
        z_ref[0, pl.ds(r, ROW_TILE), :] = z.astype(BF16)
        gr = _dot(lhs_out, wgr_ref[...]) + bgr_ref[...]
        gs_ref[0, pl.ds(r, ROW_TILE), :] = jax.nn.sigmoid(gr).astype(BF16)
        return carry

    lax.fori_loop(0, n_tiles, body, 0)


def _conv_branch(hn, w_in, b_gate, conv_w):
    b, l, d = hn.shape
    seq = l - N_META
    cc = CONV_CHUNK
    n_chunks = d // cc
    wspec = lambda first: pl.BlockSpec((d, cc), lambda bi, c: (0, first + c))
    out_spec = pl.BlockSpec((1, seq, cc), lambda bi, c: (bi, 0, c))
    return pl.pallas_call(
        _conv_kernel,
        grid=(b, n_chunks),
        in_specs=[pl.BlockSpec((1, l, d), lambda bi, c: (bi, 0, 0)),
                  wspec(1 * n_chunks), wspec(2 * n_chunks), wspec(3 * n_chunks), wspec(5 * n_chunks),
                  pl.BlockSpec((1, cc), lambda bi, c: (0, n_chunks + c)),
                  pl.BlockSpec((CONV_K, cc), lambda bi, c: (0, c))],
        out_specs=[out_spec, out_spec],
        out_shape=[jax.ShapeDtypeStruct((b, seq, d), BF16)] * 2,
        compiler_params=_params(56, ("arbitrary", "arbitrary")),
        name="conv_branch",
    )(hn, w_in, w_in, w_in, w_in, b_gate, conv_w)


def _mix_kernel(z_ref, gaya_ref, gs_ref, x_ref, cw_ref, wo_ref, g_ref, h1_ref, hn2_ref):
    yb = _dot(z_ref[...], cw_ref[...])
    mix = gaya_ref[...].astype(F32) + gs_ref[...].astype(F32) * yb
    h1 = x_ref[...] + _dot(mix.astype(BF16), wo_ref[...])
    h1_ref[...] = h1
    hn2_ref[...] = _rms(h1, g_ref[...]).astype(BF16)


def _mix(z, gaya, gs, x2d, conv_out_w, w_o, g):
    m, d = x2d.shape
    rows = pl.BlockSpec((MIX_ROWS, d), lambda i: (i, 0))
    whole = pl.BlockSpec((d, d), lambda i: (0, 0), pipeline_mode=pl.Buffered(1))
    return pl.pallas_call(
        _mix_kernel,
        grid=(m // MIX_ROWS,),
        in_specs=[rows, rows, rows, rows, whole, whole, pl.BlockSpec((1, d), lambda i: (0, 0))],
        out_specs=[rows, rows],
        out_shape=[jax.ShapeDtypeStruct((m, d), F32), jax.ShapeDtypeStruct((m, d), BF16)],
        compiler_params=_params(56, ("arbitrary",)),
        name="mix_out_proj",
    )(z, gaya, gs, x2d, conv_out_w, w_o, g)


def _ffn_kernel(hn_ref, wg_ref, wu_ref, wd_ref, g_ref, h1_hbm, out_hbm, acc, sem_in, sem_out):
    bi = pl.program_id(0)
    j = pl.program_id(1)
    seq, d = acc.shape
    n_tiles = seq // ROW_TILE

    def h1_copy(t):
        rows = pl.ds(t * ROW_TILE, ROW_TILE)
        return pltpu.make_async_copy(h1_hbm.at[bi, rows, :], acc.at[rows, :], sem_in.at[t])

    def out_copy(t):
        rows = pl.ds(t * ROW_TILE, ROW_TILE)
        return pltpu.make_async_copy(acc.at[rows, :], out_hbm.at[bi, rows, :], sem_out.at[t])

    @pl.when(j == 0)
    def _():
        for t in range(n_tiles):
            h1_copy(t).start()

    def body(t, carry):
        @pl.when(j == 0)
        def _():
            h1_copy(t).wait()

        r = pl.multiple_of(t * ROW_TILE, ROW_TILE)
        lhs = hn_ref[0, pl.ds(r, ROW_TILE), :]
        gate = _dot(lhs, wg_ref[...])
        up = _dot(lhs, wu_ref[...])
        act = (jax.nn.silu(gate) * up).astype(BF16)
        for n in range(d // DOWN_COLS):
            cols = slice(n * DOWN_COLS, (n + 1) * DOWN_COLS)
            acc[pl.ds(r, ROW_TILE), cols] += _dot(act, wd_ref[:, cols])
        return carry

    lax.fori_loop(0, n_tiles, body, 0)

    @pl.when(j == pl.num_programs(1) - 1)
    def _():
        def finish(s, carry):
            r = pl.multiple_of(s * FINISH_ROWS, FINISH_ROWS)
            acc[pl.ds(r, FINISH_ROWS), :] = _rms(acc[pl.ds(r, FINISH_ROWS), :], g_ref[...])
            return carry

        per_tile = ROW_TILE // FINISH_ROWS
        for t in range(n_tiles):
            lax.fori_loop(t * per_tile, (t + 1) * per_tile, finish, 0)
            out_copy(t).start()
        for t in range(n_tiles):
            out_copy(t).wait()


def _ffn(hn2, h1, w_gate_up, w_down, g):
    b, seq, d = hn2.shape
    fc = FFN_CHUNK
    n_chunks = FFN_HIDDEN // fc
    n_tiles = seq // ROW_TILE
    return pl.pallas_call(
        _ffn_kernel,
        grid=(b, n_chunks),
        in_specs=[pl.BlockSpec((1, seq, d), lambda bi, j: (bi, 0, 0), pipeline_mode=pl.Buffered(1)),
                  pl.BlockSpec((d, fc), lambda bi, j: (0, j)),
                  pl.BlockSpec((d, fc), lambda bi, j: (0, n_chunks + j)),
                  pl.BlockSpec((fc, d), lambda bi, j: (j, 0)),
                  pl.BlockSpec((1, d), lambda bi, j: (0, 0)),
                  pl.BlockSpec(memory_space=pl.ANY)],
        out_specs=pl.BlockSpec(memory_space=pl.ANY),
        out_shape=jax.ShapeDtypeStruct((b, seq, d), F32),
        scratch_shapes=[pltpu.VMEM((seq, d), F32),
                        pltpu.SemaphoreType.DMA((n_tiles,)),
                        pltpu.SemaphoreType.DMA((n_tiles,))],
        compiler_params=_params(60, ("arbitrary", "arbitrary")),
        name="ffn",
    )(hn2, w_gate_up, w_gate_up, w_down, g, h1)


def kernel(x, meta_tokens, norm_mix_g, w_in, b_gate, pool_w, pool_scale, conv_w, conv_out_w, w_o,
           norm_ffn_g, w_gate_up, w_down, norm_final_g):
    b, seq, d = x.shape
    row = lambda v: v.reshape(1, -1)
    hn = _norm_in(x, meta_tokens, row(norm_mix_g))
    gaya = _pool_branch(hn, w_in, row(b_gate), pool_w, row(pool_scale))
    z, gs = _conv_branch(hn, w_in, row(b_gate), conv_w)
    m = b * seq
    h1, hn2 = _mix(z.reshape(m, d), gaya.reshape(m, d), gs.reshape(m, d), x.reshape(m, d),
                   conv_out_w, w_o, row(norm_ffn_g))
    return _ffn(hn2.reshape(b, seq, d), h1.reshape(b, seq, d), w_gate_up, w_down, row(norm_final_g))
```

```python
import functools

import jax
import jax.numpy as jnp
from jax import lax
from jax.experimental import pallas as pl
from jax.experimental.pallas import tpu as pltpu

D_MODEL = 2048
N_META = 16
POOL_WINDOWS = (2, 4, 8, 16)
POOL_GROUP = D_MODEL // len(POOL_WINDOWS)
CONV_K = 3
FFN_HIDDEN = ((8 * D_MODEL // 3 + 255) // 256) * 256
EPS = 1e-6

F32 = jnp.float32
BF16 = jnp.bfloat16

MIB = 1024 * 1024
ROW_TILE = 1024
CONV_CHUNK = 256
FFN_CHUNK = 256
MIX_ROWS = 256
DOWN_COLS = 512
FINISH_ROWS = 256

assert N_META >= max(POOL_WINDOWS) and N_META >= CONV_K


def _dot(a, b):
    return lax.dot_general(a, b, (((1,), (0,)), ((), ())), preferred_element_type=F32)


def _rms(xf, g):
    ms = jnp.mean(xf * xf, axis=-1, keepdims=True)
    return xf * lax.rsqrt(ms + EPS) * g


def _params(vmem_mib, semantics):
    return pltpu.CompilerParams(dimension_semantics=semantics,
                                vmem_limit_bytes=vmem_mib * MIB)


def _norm_in_kernel(x_ref, meta_ref, g_ref, o_ref):
    i = pl.program_id(1)
    g = g_ref[...]

    @pl.when(i == 0)
    def _():
        o_ref[0, 0:N_META, :] = _rms(meta_ref[...], g).astype(BF16)

    r = pl.multiple_of(i * ROW_TILE, ROW_TILE)
    o_ref[0, pl.ds(N_META + r, ROW_TILE), :] = _rms(x_ref[0], g).astype(BF16)


def _norm_in(x, meta, g):
    b, seq, d = x.shape
    return pl.pallas_call(
        _norm_in_kernel,
        grid=(b, seq // ROW_TILE),
        in_specs=[pl.BlockSpec((1, ROW_TILE, d), lambda bi, i: (bi, i, 0)),
                  pl.BlockSpec((N_META, d), lambda bi, i: (0, 0)),
                  pl.BlockSpec((1, d), lambda bi, i: (0, 0))],
        out_specs=pl.BlockSpec((1, seq + N_META, d), lambda bi, i: (bi, 0, 0)),
        out_shape=jax.ShapeDtypeStruct((b, seq + N_META, d), BF16),
        compiler_params=_params(40, ("arbitrary", "arbitrary")),
        name="norm_in",
    )(x, meta, g)


def _pool_kernel(hn_ref, wu_ref, wga_ref, bga_ref, pw_ref, ps_ref, o_ref):
    grp = pl.program_id(1)
    n_tiles = (hn_ref.shape[1] - N_META) // ROW_TILE

    def body(window, i, carry):
        r = pl.multiple_of(i * ROW_TILE, ROW_TILE)
        lhs = hn_ref[0, pl.ds(r, ROW_TILE + N_META), :]
        u = _dot(lhs, wu_ref[...])
        s = u
        shift = 1
        while shift < window:
            s = s + pltpu.roll(s, shift, 0)
            shift *= 2
        pooled = (s * (1.0 / window) - u)[N_META:, :]
        ya = _dot(pooled.astype(BF16), pw_ref[0]) * ps_ref[...]
        ga = _dot(lhs[N_META:, :], wga_ref[...]) + bga_ref[...]
        o_ref[0, pl.ds(r, ROW_TILE), :] = (jax.nn.sigmoid(ga) * ya).astype(BF16)
        return carry

    for k, window in enumerate(POOL_WINDOWS):
        @pl.when(grp == k)
        def _(window=window):
            lax.fori_loop(0, n_tiles, functools.partial(body, window), 0)


def _pool_branch(hn, w_in, b_gate, pool_w, pool_scale):
    b, l, d = hn.shape
    seq = l - N_META
    n_grp = len(POOL_WINDOWS)
    pg = POOL_GROUP
    ga_blk = (4 * d) // pg
    return pl.pallas_call(
        _pool_kernel,
        grid=(b, n_grp),
        in_specs=[pl.BlockSpec((1, l, d), lambda bi, g: (bi, 0, 0)),
                  pl.BlockSpec((d, pg), lambda bi, g: (0, g)),
                  pl.BlockSpec((d, pg), lambda bi, g: (0, ga_blk + g)),
                  pl.BlockSpec((1, pg), lambda bi, g: (0, g)),
                  pl.BlockSpec((1, pg, pg), lambda bi, g: (g, 0, 0)),
                  pl.BlockSpec((1, pg), lambda bi, g: (0, g))],
        out_specs=pl.BlockSpec((1, seq, pg), lambda bi, g: (bi, 0, g)),
        out_shape=jax.ShapeDtypeStruct((b, seq, d), BF16),
        compiler_params=_params(56, ("arbitrary", "arbitrary")),
        name="pool_branch",
    )(hn, w_in, w_in, b_gate, pool_w, pool_scale)


def _conv_kernel(hn_ref, wgb_ref, wgc_ref, wv_ref, wgr_ref, bgr_ref, cw_ref, z_ref, gs_ref):
    w0 = cw_ref[0:1, :]
    w1 = cw_ref[1:2, :]
    w2 = cw_ref[2:3, :]
    n_tiles = (hn_ref.shape[1] - N_META) // ROW_TILE

    def body(i, carry):
        r = pl.multiple_of(i * ROW_TILE, ROW_TILE)
        lhs = hn_ref[0, pl.ds(r, ROW_TILE + N_META), :]
        lhs_out = lhs[N_META:, :]
        cv = _dot(lhs, wgc_ref[...]) * _dot(lhs, wv_ref[...])
        conv = w0 * pltpu.roll(cv, 2, 0) + w1 * pltpu.roll(cv, 1, 0) + w2 * cv
        z = _dot(lhs_out, wgb_ref[...]) * conv[N_META:, :]
        z_ref[0, pl.ds(r, ROW_TILE), :] = z.astype(BF16)
        gr = _dot(lhs_out, wgr_ref[...]) + bgr_ref[...]
        gs_ref[0, pl.ds(r, ROW_TILE), :] = jax.nn.sigmoid(gr).astype(BF16)
        return carry

    lax.fori_loop(0, n_tiles, body, 0)


def _conv_branch(hn, w_in, b_gate, conv_w):
    b, l, d = hn.shape
    seq = l - N_META
    cc = CONV_CHUNK
    n_chunks = d // cc
    wspec = lambda first: pl.BlockSpec((d, cc), lambda bi, c: (0, first + c))
    out_spec = pl.BlockSpec((1, seq, cc), lambda bi, c: (bi, 0, c))
    return pl.pallas_call(
        _conv_kernel,
        grid=(b, n_chunks),
        in_specs=[pl.BlockSpec((1, l, d), lambda bi, c: (bi, 0, 0)),
                  wspec(1 * n_chunks), wspec(2 * n_chunks), wspec(3 * n_chunks), wspec(5 * n_chunks),
                  pl.BlockSpec((1, cc), lambda bi, c: (0, n_chunks + c)),
                  pl.BlockSpec((CONV_K, cc), lambda bi, c: (0, c))],
        out_specs=[out_spec, out_spec],
        out_shape=[jax.ShapeDtypeStruct((b, seq, d), BF16)] * 2,
        compiler_params=_params(56, ("arbitrary", "arbitrary")),
        name="conv_branch",
    )(hn, w_in, w_in, w_in, w_in, b_gate, conv_w)


def _mix_kernel(z_ref, gaya_ref, gs_ref, x_ref, cw_ref, wo_ref, g_ref, h1_ref, hn2_ref):
    yb = _dot(z_ref[...], cw_ref[...])
    mix = gaya_ref[...].astype(F32) + gs_ref[...].astype(F32) * yb
    h1 = x_ref[...] + _dot(mix.astype(BF16), wo_ref[...])
    h1_ref[...] = h1
    hn2_ref[...] = _rms(h1, g_ref[...]).astype(BF16)


def _mix(z, gaya, gs, x2d, conv_out_w, w_o, g):
    m, d = x2d.shape
    rows = pl.BlockSpec((MIX_ROWS, d), lambda i: (i, 0))
    whole = pl.BlockSpec((d, d), lambda i: (0, 0), pipeline_mode=pl.Buffered(1))
    return pl.pallas_call(
        _mix_kernel,
        grid=(m // MIX_ROWS,),
        in_specs=[rows, rows, rows, rows, whole, whole, pl.BlockSpec((1, d), lambda i: (0, 0))],
        out_specs=[rows, rows],
        out_shape=[jax.ShapeDtypeStruct((m, d), F32), jax.ShapeDtypeStruct((m, d), BF16)],
        compiler_params=_params(56, ("arbitrary",)),
        name="mix_out_proj",
    )(z, gaya, gs, x2d, conv_out_w, w_o, g)


def _ffn_kernel(hn_ref, wg_ref, wu_ref, wd_ref, g_ref, h1_hbm, out_hbm, acc, sem_in, sem_out):
    bi = pl.program_id(0)
    j = pl.program_id(1)
    seq, d = acc.shape
    n_tiles = seq // ROW_TILE

    def h1_copy(t):
        rows = pl.ds(t * ROW_TILE, ROW_TILE)
        return pltpu.make_async_copy(h1_hbm.at[bi, rows, :], acc.at[rows, :], sem_in.at[t])

    def out_copy(t):
        rows = pl.ds(t * ROW_TILE, ROW_TILE)
        return pltpu.make_async_copy(acc.at[rows, :], out_hbm.at[bi, rows, :], sem_out.at[t])

    @pl.when(j == 0)
    def _():
        for t in range(n_tiles):
            h1_copy(t).start()

    def body(t, carry):
        @pl.when(j == 0)
        def _():
            h1_copy(t).wait()

        r = pl.multiple_of(t * ROW_TILE, ROW_TILE)
        lhs = hn_ref[0, pl.ds(r, ROW_TILE), :]
        gate = _dot(lhs, wg_ref[...])
        up = _dot(lhs, wu_ref[...])
        act = (jax.nn.silu(gate) * up).astype(BF16)
        for n in range(d // DOWN_COLS):
            cols = slice(n * DOWN_COLS, (n + 1) * DOWN_COLS)
            acc[pl.ds(r, ROW_TILE), cols] += _dot(act, wd_ref[:, cols])
        return carry

    lax.fori_loop(0, n_tiles, body, 0)

    @pl.when(j == pl.num_programs(1) - 1)
    def _():
        def finish(s, carry):
            r = pl.multiple_of(s * FINISH_ROWS, FINISH_ROWS)
            acc[pl.ds(r, FINISH_ROWS), :] = _rms(acc[pl.ds(r, FINISH_ROWS), :], g_ref[...])
            return carry

        per_tile = ROW_TILE // FINISH_ROWS
        for t in range(n_tiles):
            lax.fori_loop(t * per_tile, (t + 1) * per_tile, finish, 0)
            out_copy(t).start()
        for t in range(n_tiles):
            out_copy(t).wait()


def _ffn(hn2, h1, w_gate_up, w_down, g):
    b, seq, d = hn2.shape
    fc = FFN_CHUNK
    n_chunks = FFN_HIDDEN // fc
    n_tiles = seq // ROW_TILE
    return pl.pallas_call(
        _ffn_kernel,
        grid=(b, n_chunks),
        in_specs=[pl.BlockSpec((1, seq, d), lambda bi, j: (bi, 0, 0), pipeline_mode=pl.Buffered(1)),
                  pl.BlockSpec((d, fc), lambda bi, j: (0, j)),
                  pl.BlockSpec((d, fc), lambda bi, j: (0, n_chunks + j)),
                  pl.BlockSpec((fc, d), lambda bi, j: (j, 0)),
                  pl.BlockSpec((1, d), lambda bi, j: (0, 0)),
                  pl.BlockSpec(memory_space=pl.ANY)],
        out_specs=pl.BlockSpec(memory_space=pl.ANY),
        out_shape=jax.ShapeDtypeStruct((b, seq, d), F32),
        scratch_shapes=[pltpu.VMEM((seq, d), F32),
                        pltpu.SemaphoreType.DMA((n_tiles,)),
                        pltpu.SemaphoreType.DMA((n_tiles,))],
        compiler_params=_params(60, ("arbitrary", "arbitrary")),
        name="ffn",
    )(hn2, w_gate_up, w_gate_up, w_down, g, h1)


def kernel(x, meta_tokens, norm_mix_g, w_in, b_gate, pool_w, pool_scale, conv_w, conv_out_w, w_o,
           norm_ffn_g, w_gate_up, w_down, norm_final_g):
    b, seq, d = x.shape
    row = lambda v: v.reshape(1, -1)
    hn = _norm_in(x, meta_tokens, row(norm_mix_g))
    gaya = _pool_branch(hn, w_in, row(b_gate), pool_w, row(pool_scale))
    z, gs = _conv_branch(hn, w_in, row(b_gate), conv_w)
    m = b * seq
    h1, hn2 = _mix(z.reshape(m, d), gaya.reshape(m, d), gs.reshape(m, d), x.reshape(m, d),
                   conv_out_w, w_o, row(norm_ffn_g))
    return _ffn(hn2.reshape(b, seq, d), h1.reshape(b, seq, d), w_gate_up, w_down, row(norm_final_g))
```

```python
import functools

import jax
import jax.numpy as jnp
from jax import lax
from jax.experimental import pallas as pl
from jax.experimental.pallas import tpu as pltpu

D_MODEL = 2048
N_META = 16
POOL_WINDOWS = (2, 4, 8, 16)
POOL_GROUP = D_MODEL // len(POOL_WINDOWS)
CONV_K = 3
FFN_HIDDEN = ((8 * D_MODEL // 3 + 255) // 256) * 256
EPS = 1e-6

F32 = jnp.float32
BF16 = jnp.bfloat16

MIB = 1024 * 1024
ROW_TILE = 1024
CONV_CHUNK = 256
FFN_CHUNK = 512
MIX_ROWS = 256
DOWN_COLS = 512
FINISH_ROWS = 256

assert N_META >= max(POOL_WINDOWS) and N_META >= CONV_K


def _dot(a, b):
    return lax.dot_general(a, b, (((1,), (0,)), ((), ())), preferred_element_type=F32)


def _rms(xf, g):
    ms = jnp.mean(xf * xf, axis=-1, keepdims=True)
    return xf * lax.rsqrt(ms + EPS) * g


def _params(vmem_mib, semantics):
    return pltpu.CompilerParams(dimension_semantics=semantics,
                                vmem_limit_bytes=vmem_mib * MIB)


def _norm_in_kernel(x_ref, meta_ref, g_ref, o_ref):
    i = pl.program_id(1)
    g = g_ref[...]

    @pl.when(i == 0)
    def _():
        o_ref[0, 0:N_META, :] = _rms(meta_ref[...], g).astype(BF16)

    r = pl.multiple_of(i * ROW_TILE, ROW_TILE)
    o_ref[0, pl.ds(N_META + r, ROW_TILE), :] = _rms(x_ref[0], g).astype(BF16)


def _norm_in(x, meta, g):
    b, seq, d = x.shape
    return pl.pallas_call(
        _norm_in_kernel,
        grid=(b, seq // ROW_TILE),
        in_specs=[pl.BlockSpec((1, ROW_TILE, d), lambda bi, i: (bi, i, 0)),
                  pl.BlockSpec((N_META, d), lambda bi, i: (0, 0)),
                  pl.BlockSpec((1, d), lambda bi, i: (0, 0))],
        out_specs=pl.BlockSpec((1, seq + N_META, d), lambda bi, i: (bi, 0, 0)),
        out_shape=jax.ShapeDtypeStruct((b, seq + N_META, d), BF16),
        compiler_params=_params(40, ("arbitrary", "arbitrary")),
        name="norm_in",
    )(x, meta, g)


def _pool_kernel(hn_ref, wu_ref, wga_ref, bga_ref, pw_ref, ps_ref, o_ref):
    grp = pl.program_id(1)
    n_tiles = (hn_ref.shape[1] - N_META) // ROW_TILE

    def body(window, i, carry):
        r = pl.multiple_of(i * ROW_TILE, ROW_TILE)
        lhs = hn_ref[0, pl.ds(r, ROW_TILE + N_META), :]
        u = _dot(lhs, wu_ref[...])
        s = u
        shift = 1
        while shift < window:
            s = s + pltpu.roll(s, shift, 0)
            shift *= 2
        pooled = (s * (1.0 / window) - u)[N_META:, :]
        ya = _dot(pooled.astype(BF16), pw_ref[0]) * ps_ref[...]
        ga = _dot(lhs[N_META:, :], wga_ref[...]) + bga_ref[...]
        o_ref[0, pl.ds(r, ROW_TILE), :] = (jax.nn.sigmoid(ga) * ya).astype(BF16)
        return carry

    for k, window in enumerate(POOL_WINDOWS):
        @pl.when(grp == k)
        def _(window=window):
            lax.fori_loop(0, n_tiles, functools.partial(body, window), 0)


def _pool_branch(hn, w_in, b_gate, pool_w, pool_scale):
    b, l, d = hn.shape
    seq = l - N_META
    n_grp = len(POOL_WINDOWS)
    pg = POOL_GROUP
    ga_blk = (4 * d) // pg
    return pl.pallas_call(
        _pool_kernel,
        grid=(b, n_grp),
        in_specs=[pl.BlockSpec((1, l, d), lambda bi, g: (bi, 0, 0)),
                  pl.BlockSpec((d, pg), lambda bi, g: (0, g)),
                  pl.BlockSpec((d, pg), lambda bi, g: (0, ga_blk + g)),
                  pl.BlockSpec((1, pg), lambda bi, g: (0, g)),
                  pl.BlockSpec((1, pg, pg), lambda bi, g: (g, 0, 0)),
                  pl.BlockSpec((1, pg), lambda bi, g: (0, g))],
        out_specs=pl.BlockSpec((1, seq, pg), lambda bi, g: (bi, 0, g)),
        out_shape=jax.ShapeDtypeStruct((b, seq, d), BF16),
        compiler_params=_params(56, ("arbitrary", "arbitrary")),
        name="pool_branch",
    )(hn, w_in, w_in, b_gate, pool_w, pool_scale)


def _conv_kernel(hn_ref, wgb_ref, wgc_ref, wv_ref, wgr_ref, bgr_ref, cw_ref, z_ref, gs_ref):
    w0 = cw_ref[0:1, :]
    w1 = cw_ref[1:2, :]
    w2 = cw_ref[2:3, :]
    n_tiles = (hn_ref.shape[1] - N_META) // ROW_TILE

    def body(i, carry):
        r = pl.multiple_of(i * ROW_TILE, ROW_TILE)
        lhs = hn_ref[0, pl.ds(r, ROW_TILE + N_META), :]
        lhs_out = lhs[N_META:, :]
        cv = _dot(lhs, wgc_ref[...]) * _dot(lhs, wv_ref[...])
        conv = w0 * pltpu.roll(cv, 2, 0) + w1 * pltpu.roll(cv, 1, 0) + w2 * cv
        z = _dot(lhs_out, wgb_ref[...]) * conv[N_META:, :]
        z_ref[0, pl.ds(r, ROW_TILE), :] = z.astype(BF16)
        gr = _dot(lhs_out, wgr_ref[...]) + bgr_ref[...]
        gs_ref[0, pl.ds(r, ROW_TILE), :] = jax.nn.sigmoid(gr).astype(BF16)
        return carry

    lax.fori_loop(0, n_tiles, body, 0)


def _conv_branch(hn, w_in, b_gate, conv_w):
    b, l, d = hn.shape
    seq = l - N_META
    cc = CONV_CHUNK
    n_chunks = d // cc
    wspec = lambda first: pl.BlockSpec((d, cc), lambda bi, c: (0, first + c))
    out_spec = pl.BlockSpec((1, seq, cc), lambda bi, c: (bi, 0, c))
    return pl.pallas_call(
        _conv_kernel,
        grid=(b, n_chunks),
        in_specs=[pl.BlockSpec((1, l, d), lambda bi, c: (bi, 0, 0)),
                  wspec(1 * n_chunks), wspec(2 * n_chunks), wspec(3 * n_chunks), wspec(5 * n_chunks),
                  pl.BlockSpec((1, cc), lambda bi, c: (0, n_chunks + c)),
                  pl.BlockSpec((CONV_K, cc), lambda bi, c: (0, c))],
        out_specs=[out_spec, out_spec],
        out_shape=[jax.ShapeDtypeStruct((b, seq, d), BF16)] * 2,
        compiler_params=_params(56, ("arbitrary", "arbitrary")),
        name="conv_branch",
    )(hn, w_in, w_in, w_in, w_in, b_gate, conv_w)


def _mix_kernel(z_ref, gaya_ref, gs_ref, x_ref, cw_ref, wo_ref, g_ref, h1_ref, hn2_ref):
    yb = _dot(z_ref[...], cw_ref[...])
    mix = gaya_ref[...].astype(F32) + gs_ref[...].astype(F32) * yb
    h1 = x_ref[...] + _dot(mix.astype(BF16), wo_ref[...])
    h1_ref[...] = h1
    hn2_ref[...] = _rms(h1, g_ref[...]).astype(BF16)


def _mix(z, gaya, gs, x2d, conv_out_w, w_o, g):
    m, d = x2d.shape
    rows = pl.BlockSpec((MIX_ROWS, d), lambda i: (i, 0))
    whole = pl.BlockSpec((d, d), lambda i: (0, 0), pipeline_mode=pl.Buffered(1))
    return pl.pallas_call(
        _mix_kernel,
        grid=(m // MIX_ROWS,),
        in_specs=[rows, rows, rows, rows, whole, whole, pl.BlockSpec((1, d), lambda i: (0, 0))],
        out_specs=[rows, rows],
        out_shape=[jax.ShapeDtypeStruct((m, d), F32), jax.ShapeDtypeStruct((m, d), BF16)],
        compiler_params=_params(56, ("arbitrary",)),
        name="mix_out_proj",
    )(z, gaya, gs, x2d, conv_out_w, w_o, g)


def _ffn_kernel(wg_ref, wu_ref, wd_ref, g_ref, hn_hbm, h1_hbm, out_hbm,
                hn_s, acc, stage, sem_hn, sem_h1, sem_out):
    bi = pl.program_id(0)
    j = pl.program_id(1)
    last_b = pl.num_programs(0) - 1
    last_j = pl.num_programs(1) - 1
    seq, d = acc.shape
    n_tiles = seq // ROW_TILE
    n_fin = seq // FINISH_ROWS
    fin_per_tile = ROW_TILE // FINISH_ROWS

    def hn_copy(batch, t):
        rows = pl.ds(t * ROW_TILE, ROW_TILE)
        return pltpu.make_async_copy(hn_hbm.at[batch, rows, :], hn_s.at[rows, :], sem_hn.at[t])

    def h1_copy(s):
        rows = pl.ds(s * FINISH_ROWS, FINISH_ROWS)
        return pltpu.make_async_copy(h1_hbm.at[bi, rows, :], stage.at[s % 2], sem_h1.at[s % 2])

    def out_copy(batch, t):
        rows = pl.ds(t * ROW_TILE, ROW_TILE)
        return pltpu.make_async_copy(acc.at[rows, :], out_hbm.at[batch, rows, :], sem_out.at[t])

    @pl.when((bi == 0) & (j == 0))
    def _():
        for t in range(n_tiles):
            hn_copy(0, t).start()
        acc[...] = jnp.zeros_like(acc)

    @pl.when(j == last_j)
    def _():
        for s in range(2):
            h1_copy(s).start()

    for t in range(n_tiles):
        rows = pl.ds(t * ROW_TILE, ROW_TILE)

        @pl.when(j == 0)
        def _():
            hn_copy(bi, t).wait()

            @pl.when(bi > 0)
            def _():
                out_copy(bi - 1, t).wait()

        lhs = hn_s[rows, :]
        gate = _dot(lhs, wg_ref[...])
        up = _dot(lhs, wu_ref[...])
        act = (jax.nn.silu(gate) * up).astype(BF16)
        for n in range(d // DOWN_COLS):
            cols = slice(n * DOWN_COLS, (n + 1) * DOWN_COLS)
            prev = jnp.where(j == 0, 0.0, acc[rows, cols])
            acc[rows, cols] = prev + _dot(act, wd_ref[:, cols])

        @pl.when((j == last_j) & (bi < last_b))
        def _():
            hn_copy(bi + 1, t).start()

    @pl.when(j == last_j)
    def _():
        for s in range(n_fin):
            rows = pl.ds(s * FINISH_ROWS, FINISH_ROWS)
            h1_copy(s).wait()
            acc[rows, :] = _rms(acc[rows, :] + stage[s % 2], g_ref[...])
            if s + 2 < n_fin:
                h1_copy(s + 2).start()
            if (s + 1) % fin_per_tile == 0:
                out_copy(bi, s // fin_per_tile).start()

        @pl.when(bi == last_b)
        def _():
            for t in range(n_tiles):
                out_copy(bi, t).wait()


def _ffn(hn2, h1, w_gate_up, w_down, g):
    b, seq, d = hn2.shape
    fc = FFN_CHUNK
    n_chunks = FFN_HIDDEN // fc
    n_tiles = seq // ROW_TILE
    return pl.pallas_call(
        _ffn_kernel,
        grid=(b, n_chunks),
        in_specs=[pl.BlockSpec((d, fc), lambda bi, j: (0, j)),
                  pl.BlockSpec((d, fc), lambda bi, j: (0, n_chunks + j)),
                  pl.BlockSpec((fc, d), lambda bi, j: (j, 0)),
                  pl.BlockSpec((1, d), lambda bi, j: (0, 0)),
                  pl.BlockSpec(memory_space=pl.ANY),
                  pl.BlockSpec(memory_space=pl.ANY)],
        out_specs=pl.BlockSpec(memory_space=pl.ANY),
        out_shape=jax.ShapeDtypeStruct((b, seq, d), F32),
        scratch_shapes=[pltpu.VMEM((seq, d), BF16),
                        pltpu.VMEM((seq, d), F32),
                        pltpu.VMEM((2, FINISH_ROWS, d), F32),
                        pltpu.SemaphoreType.DMA((n_tiles,)),
                        pltpu.SemaphoreType.DMA((2,)),
                        pltpu.SemaphoreType.DMA((n_tiles,))],
        compiler_params=_params(60, ("arbitrary", "arbitrary")),
        name="ffn",
    )(w_gate_up, w_gate_up, w_down, g, hn2, h1)


def kernel(x, meta_tokens, norm_mix_g, w_in, b_gate, pool_w, pool_scale, conv_w, conv_out_w, w_o,
           norm_ffn_g, w_gate_up, w_down, norm_final_g):
    b, seq, d = x.shape
    row = lambda v: v.reshape(1, -1)
    hn = _norm_in(x, meta_tokens, row(norm_mix_g))
    gaya = _pool_branch(hn, w_in, row(b_gate), pool_w, row(pool_scale))
    z, gs = _conv_branch(hn, w_in, row(b_gate), conv_w)
    m = b * seq
    h1, hn2 = _mix(z.reshape(m, d), gaya.reshape(m, d), gs.reshape(m, d), x.reshape(m, d),
                   conv_out_w, w_o, row(norm_ffn_g))
    return _ffn(hn2.reshape(b, seq, d), h1.reshape(b, seq, d), w_gate_up, w_down, row(norm_final_g))
```

```python
import jax
import jax.numpy as jnp
from jax import lax
from jax.experimental import pallas as pl
from jax.experimental.pallas import tpu as pltpu

D_MODEL = 2048
N_META = 16
POOL_WINDOWS = (2, 4, 8, 16)
POOL_GROUP = D_MODEL // len(POOL_WINDOWS)
CONV_K = 3
FFN_HIDDEN = ((8 * D_MODEL // 3 + 255) // 256) * 256
EPS = 1e-6

F32 = jnp.float32
BF16 = jnp.bfloat16

MIB = 1024 * 1024
ROW_TILE = 1024
IN_CHUNK = 256
FFN_CHUNK = 512
MIX_ROWS = 256
DOWN_COLS = 512
FINISH_ROWS = 256

assert N_META >= max(POOL_WINDOWS) and N_META >= CONV_K
assert POOL_WINDOWS == (2, 4, 8, 16)


def _dot(a, b):
    return lax.dot_general(a, b, (((1,), (0,)), ((), ())), preferred_element_type=F32)


def _rms(xf, g):
    ms = jnp.mean(xf * xf, axis=-1, keepdims=True)
    return xf * lax.rsqrt(ms + EPS) * g


def _params(vmem_mib, semantics):
    return pltpu.CompilerParams(dimension_semantics=semantics,
                                vmem_limit_bytes=vmem_mib * MIB)


def _norm_in_kernel(x_ref, meta_ref, g_ref, o_ref):
    i = pl.program_id(1)
    g = g_ref[...]

    @pl.when(i == 0)
    def _():
        o_ref[0, 0:N_META, :] = _rms(meta_ref[...], g).astype(BF16)

    r = pl.multiple_of(i * ROW_TILE, ROW_TILE)
    o_ref[0, pl.ds(N_META + r, ROW_TILE), :] = _rms(x_ref[0], g).astype(BF16)


def _norm_in(x, meta, g):
    b, seq, d = x.shape
    return pl.pallas_call(
        _norm_in_kernel,
        grid=(b, seq // ROW_TILE),
        in_specs=[pl.BlockSpec((1, ROW_TILE, d), lambda bi, i: (bi, i, 0)),
                  pl.BlockSpec((N_META, d), lambda bi, i: (0, 0)),
                  pl.BlockSpec((1, d), lambda bi, i: (0, 0))],
        out_specs=pl.BlockSpec((1, seq + N_META, d), lambda bi, i: (bi, 0, 0)),
        out_shape=jax.ShapeDtypeStruct((b, seq + N_META, d), BF16),
        compiler_params=_params(40, ("arbitrary", "arbitrary")),
        name="norm_in",
    )(x, meta, g)


def _fold_pool_kernel(wu_ref, pw_ref, ps_ref, o_ref):
    o_ref[...] = (_dot(wu_ref[...], pw_ref[0]) * ps_ref[...]).astype(BF16)


def _fold_pool(w_in, pool_w, pool_scale):
    d = w_in.shape[0]
    pg = POOL_GROUP
    return pl.pallas_call(
        _fold_pool_kernel,
        grid=(len(POOL_WINDOWS),),
        in_specs=[pl.BlockSpec((d, pg), lambda g: (0, g)),
                  pl.BlockSpec((1, pg, pg), lambda g: (g, 0, 0)),
                  pl.BlockSpec((1, pg), lambda g: (0, g))],
        out_specs=pl.BlockSpec((d, pg), lambda g: (0, g)),
        out_shape=jax.ShapeDtypeStruct((d, d), BF16),
        compiler_params=_params(32, ("arbitrary",)),
        name="fold_pool",
    )(w_in, pool_w, pool_scale)


def _inproj_kernel(hn_ref, wp_ref, wga_ref, wgb_ref, wgc_ref, wv_ref, wgr_ref, bga_ref, bgr_ref, cw_ref,
                   gaya_ref, z_ref, gs_ref):
    chunk = pl.program_id(1)
    grp = chunk // (POOL_GROUP // IN_CHUNK)
    w0 = cw_ref[0:1, :]
    w1 = cw_ref[1:2, :]
    w2 = cw_ref[2:3, :]
    inv_window = jnp.where(grp == 0, 0.5, jnp.where(grp == 1, 0.25, jnp.where(grp == 2, 0.125, 0.0625)))
    n_tiles = (hn_ref.shape[1] - N_META) // ROW_TILE

    def body(i, carry):
        r = pl.multiple_of(i * ROW_TILE, ROW_TILE)
        lhs = hn_ref[0, pl.ds(r, ROW_TILE + N_META), :]
        lhs_out = lhs[N_META:, :]

        y = _dot(lhs, wp_ref[...])
        s = y + pltpu.roll(y, 1, 0)
        s = jnp.where(grp >= 1, s + pltpu.roll(s, 2, 0), s)
        s = jnp.where(grp >= 2, s + pltpu.roll(s, 4, 0), s)
        s = jnp.where(grp >= 3, s + pltpu.roll(s, 8, 0), s)
        pooled = (s * inv_window - y)[N_META:, :]
        ga = _dot(lhs_out, wga_ref[...]) + bga_ref[...]
        gaya_ref[0, pl.ds(r, ROW_TILE), :] = (jax.nn.sigmoid(ga) * pooled).astype(BF16)

        cv = _dot(lhs, wgc_ref[...]) * _dot(lhs, wv_ref[...])
        conv = w0 * pltpu.roll(cv, 2, 0) + w1 * pltpu.roll(cv, 1, 0) + w2 * cv
        z = _dot(lhs_out, wgb_ref[...]) * conv[N_META:, :]
        z_ref[0, pl.ds(r, ROW_TILE), :] = z.astype(BF16)
        gr = _dot(lhs_out, wgr_ref[...]) + bgr_ref[...]
        gs_ref[0, pl.ds(r, ROW_TILE), :] = jax.nn.sigmoid(gr).astype(BF16)
        return carry

    lax.fori_loop(0, n_tiles, body, 0)


def _inproj(hn, w_pool, w_in, b_gate, conv_w):
    b, l, d = hn.shape
    seq = l - N_META
    cc = IN_CHUNK
    n_chunks = d // cc
    wspec = lambda first: pl.BlockSpec((d, cc), lambda bi, c: (0, first + c))
    bspec = lambda first: pl.BlockSpec((1, cc), lambda bi, c: (0, first + c))
    out_spec = pl.BlockSpec((1, seq, cc), lambda bi, c: (bi, 0, c))
    return pl.pallas_call(
        _inproj_kernel,
        grid=(b, n_chunks),
        in_specs=[pl.BlockSpec((1, l, d), lambda bi, c: (bi, 0, 0)),
                  wspec(0),
                  wspec(4 * n_chunks), wspec(1 * n_chunks), wspec(2 * n_chunks), wspec(3 * n_chunks),
                  wspec(5 * n_chunks),
                  bspec(0), bspec(n_chunks),
                  pl.BlockSpec((CONV_K, cc), lambda bi, c: (0, c))],
        out_specs=[out_spec, out_spec, out_spec],
        out_shape=[jax.ShapeDtypeStruct((b, seq, d), BF16)] * 3,
        compiler_params=_params(60, ("arbitrary", "arbitrary")),
        name="inproj_mixers",
    )(hn, w_pool, w_in, w_in, w_in, w_in, w_in, b_gate, b_gate, conv_w)


def _mix_kernel(z_ref, gaya_ref, gs_ref, x_ref, cw_ref, wo_ref, g_ref, h1_ref, hn2_ref):
    yb = _dot(z_ref[...], cw_ref[...])
    mix = gaya_ref[...].astype(F32) + gs_ref[...].astype(F32) * yb
    h1 = x_ref[...] + _dot(mix.astype(BF16), wo_ref[...])
    h1_ref[...] = h1
    hn2_ref[...] = _rms(h1, g_ref[...]).astype(BF16)


def _mix(z, gaya, gs, x2d, conv_out_w, w_o, g):
    m, d = x2d.shape
    rows = pl.BlockSpec((MIX_ROWS, d), lambda i: (i, 0))
    whole = pl.BlockSpec((d, d), lambda i: (0, 0), pipeline_mode=pl.Buffered(1))
    return pl.pallas_call(
        _mix_kernel,
        grid=(m // MIX_ROWS,),
        in_specs=[rows, rows, rows, rows, whole, whole, pl.BlockSpec((1, d), lambda i: (0, 0))],
        out_specs=[rows, rows],
        out_shape=[jax.ShapeDtypeStruct((m, d), F32), jax.ShapeDtypeStruct((m, d), BF16)],
        compiler_params=_params(56, ("arbitrary",)),
        name="mix_out_proj",
    )(z, gaya, gs, x2d, conv_out_w, w_o, g)


def _ffn_kernel(wg_ref, wu_ref, wd_ref, g_ref, hn_hbm, h1_hbm, out_hbm,
                hn_s, acc, stage, sem_hn, sem_h1, sem_out):
    bi = pl.program_id(0)
    j = pl.program_id(1)
    last_b = pl.num_programs(0) - 1
    last_j = pl.num_programs(1) - 1
    seq, d = acc.shape
    n_tiles = seq // ROW_TILE
    n_fin = seq // FINISH_ROWS
    fin_per_tile = ROW_TILE // FINISH_ROWS

    def hn_copy(batch, t):
        rows = pl.ds(t * ROW_TILE, ROW_TILE)
        return pltpu.make_async_copy(hn_hbm.at[batch, rows, :], hn_s.at[rows, :], sem_hn.at[t])

    def h1_copy(s):
        rows = pl.ds(s * FINISH_ROWS, FINISH_ROWS)
        return pltpu.make_async_copy(h1_hbm.at[bi, rows, :], stage.at[s % 2], sem_h1.at[s % 2])

    def out_copy(batch, t):
        rows = pl.ds(t * ROW_TILE, ROW_TILE)
        return pltpu.make_async_copy(acc.at[rows, :], out_hbm.at[batch, rows, :], sem_out.at[t])

    @pl.when((bi == 0) & (j == 0))
    def _():
        for t in range(n_tiles):
            hn_copy(0, t).start()
        acc[...] = jnp.zeros_like(acc)

    @pl.when(j < n_fin)
    def _():
        h1_copy(j).start()

    for t in range(n_tiles):
        rows = pl.ds(t * ROW_TILE, ROW_TILE)

        @pl.when(j == 0)
        def _():
            hn_copy(bi, t).wait()

            @pl.when(bi > 0)
            def _():
                out_copy(bi - 1, t).wait()

        lhs = hn_s[rows, :]
        gate = _dot(lhs, wg_ref[...])
        up = _dot(lhs, wu_ref[...])
        act = (jax.nn.silu(gate) * up).astype(BF16)
        for n in range(d // DOWN_COLS):
            cols = slice(n * DOWN_COLS, (n + 1) * DOWN_COLS)
            prev = jnp.where(j == 0, 0.0, acc[rows, cols])
            acc[rows, cols] = prev + _dot(act, wd_ref[:, cols])

        @pl.when((j == last_j) & (bi < last_b))
        def _():
            hn_copy(bi + 1, t).start()

    @pl.when((j >= 1) & (j <= n_fin))
    def _():
        piece = j - 1
        h1_copy(piece).wait()
        rows = pl.ds(pl.multiple_of(piece * FINISH_ROWS, FINISH_ROWS), FINISH_ROWS)
        acc[rows, :] += stage[piece % 2]

    @pl.when(j == last_j)
    def _():
        for s in range(n_fin):
            rows = pl.ds(s * FINISH_ROWS, FINISH_ROWS)
            acc[rows, :] = _rms(acc[rows, :], g_ref[...])
            if (s + 1) % fin_per_tile == 0:
                out_copy(bi, s // fin_per_tile).start()

        @pl.when(bi == last_b)
        def _():
            for t in range(n_tiles):
                out_copy(bi, t).wait()


def _ffn(hn2, h1, w_gate_up, w_down, g):
    b, seq, d = hn2.shape
    fc = FFN_CHUNK
    n_chunks = FFN_HIDDEN // fc
    n_tiles = seq // ROW_TILE
    assert n_chunks > seq // FINISH_ROWS
    return pl.pallas_call(
        _ffn_kernel,
        grid=(b, n_chunks),
        in_specs=[pl.BlockSpec((d, fc), lambda bi, j: (0, j)),
                  pl.BlockSpec((d, fc), lambda bi, j: (0, n_chunks + j)),
                  pl.BlockSpec((fc, d), lambda bi, j: (j, 0)),
                  pl.BlockSpec((1, d), lambda bi, j: (0, 0)),
                  pl.BlockSpec(memory_space=pl.ANY),
                  pl.BlockSpec(memory_space=pl.ANY)],
        out_specs=pl.BlockSpec(memory_space=pl.ANY),
        out_shape=jax.ShapeDtypeStruct((b, seq, d), F32),
        scratch_shapes=[pltpu.VMEM((seq, d), BF16),
                        pltpu.VMEM((seq, d), F32),
                        pltpu.VMEM((2, FINISH_ROWS, d), F32),
                        pltpu.SemaphoreType.DMA((n_tiles,)),
                        pltpu.SemaphoreType.DMA((2,)),
                        pltpu.SemaphoreType.DMA((n_tiles,))],
        compiler_params=_params(60, ("arbitrary", "arbitrary")),
        name="ffn",
    )(w_gate_up, w_gate_up, w_down, g, hn2, h1)


def kernel(x, meta_tokens, norm_mix_g, w_in, b_gate, pool_w, pool_scale, conv_w, conv_out_w, w_o,
           norm_ffn_g, w_gate_up, w_down, norm_final_g):
    b, seq, d = x.shape
    row = lambda v: v.reshape(1, -1)
    hn = _norm_in(x, meta_tokens, row(norm_mix_g))
    w_pool = _fold_pool(w_in, pool_w, row(pool_scale))
    gaya, z, gs = _inproj(hn, w_pool, w_in, row(b_gate), conv_w)
    m = b * seq
    h1, hn2 = _mix(z.reshape(m, d), gaya.reshape(m, d), gs.reshape(m, d), x.reshape(m, d),
                   conv_out_w, w_o, row(norm_ffn_g))
    return _ffn(hn2.reshape(b, seq, d), h1.reshape(b, seq, d), w_gate_up, w_down, row(norm_final_g))
```

```python
import jax
import jax.numpy as jnp
from jax import lax
from jax.experimental import pallas as pl
from jax.experimental.pallas import tpu as pltpu

D_MODEL = 2048
N_META = 16
POOL_WINDOWS = (2, 4, 8, 16)
POOL_GROUP = D_MODEL // len(POOL_WINDOWS)
CONV_K = 3
FFN_HIDDEN = ((8 * D_MODEL // 3 + 255) // 256) * 256
EPS = 1e-6

F32 = jnp.float32
BF16 = jnp.bfloat16

MIB = 1024 * 1024
ROW_TILE = 1024
IN_CHUNK = 256
FFN_CHUNK = 512
MIX_ROWS = 512
DOWN_COLS = 512
FINISH_ROWS = 256

assert N_META >= max(POOL_WINDOWS) and N_META >= CONV_K
assert POOL_WINDOWS == (2, 4, 8, 16)


def _dot(a, b):
    return lax.dot_general(a, b, (((1,), (0,)), ((), ())), preferred_element_type=F32)


def _rms(xf, g):
    ms = jnp.mean(xf * xf, axis=-1, keepdims=True)
    return xf * lax.rsqrt(ms + EPS) * g


def _params(vmem_mib, semantics):
    return pltpu.CompilerParams(dimension_semantics=semantics,
                                vmem_limit_bytes=vmem_mib * MIB)


def _norm_in_kernel(x_ref, meta_ref, g_ref, o_ref):
    i = pl.program_id(1)
    g = g_ref[...]

    @pl.when(i == 0)
    def _():
        o_ref[0, 0:N_META, :] = _rms(meta_ref[...], g).astype(BF16)

    r = pl.multiple_of(i * ROW_TILE, ROW_TILE)
    o_ref[0, pl.ds(N_META + r, ROW_TILE), :] = _rms(x_ref[0], g).astype(BF16)


def _norm_in(x, meta, g):
    b, seq, d = x.shape
    return pl.pallas_call(
        _norm_in_kernel,
        grid=(b, seq // ROW_TILE),
        in_specs=[pl.BlockSpec((1, ROW_TILE, d), lambda bi, i: (bi, i, 0)),
                  pl.BlockSpec((N_META, d), lambda bi, i: (0, 0)),
                  pl.BlockSpec((1, d), lambda bi, i: (0, 0))],
        out_specs=pl.BlockSpec((1, seq + N_META, d), lambda bi, i: (bi, 0, 0)),
        out_shape=jax.ShapeDtypeStruct((b, seq + N_META, d), BF16),
        compiler_params=_params(40, ("arbitrary", "arbitrary")),
        name="norm_in",
    )(x, meta, g)


def _fold_pool_kernel(wu_ref, pw_ref, ps_ref, o_ref):
    o_ref[...] = (_dot(wu_ref[...], pw_ref[0]) * ps_ref[...]).astype(BF16)


def _fold_pool(w_in, pool_w, pool_scale):
    d = w_in.shape[0]
    pg = POOL_GROUP
    return pl.pallas_call(
        _fold_pool_kernel,
        grid=(len(POOL_WINDOWS),),
        in_specs=[pl.BlockSpec((d, pg), lambda g: (0, g)),
                  pl.BlockSpec((1, pg, pg), lambda g: (g, 0, 0)),
                  pl.BlockSpec((1, pg), lambda g: (0, g))],
        out_specs=pl.BlockSpec((d, pg), lambda g: (0, g)),
        out_shape=jax.ShapeDtypeStruct((d, d), BF16),
        compiler_params=_params(32, ("arbitrary",)),
        name="fold_pool",
    )(w_in, pool_w, pool_scale)


def _inproj_kernel(hn_ref, wp_ref, wga_ref, wgb_ref, wgc_ref, wv_ref, wgr_ref, bga_ref, bgr_ref, cw_ref,
                   cow_ref, wo_ref, gaya_ref, z_ref, gs_ref, cow_bf_ref, wo_bf_ref):
    chunk = pl.program_id(1)
    cow_bf_ref[...] = cow_ref[...].astype(BF16)
    wo_bf_ref[...] = wo_ref[...].astype(BF16)
    grp = chunk // (POOL_GROUP // IN_CHUNK)
    w0 = cw_ref[0:1, :]
    w1 = cw_ref[1:2, :]
    w2 = cw_ref[2:3, :]
    inv_window = jnp.where(grp == 0, 0.5, jnp.where(grp == 1, 0.25, jnp.where(grp == 2, 0.125, 0.0625)))
    n_tiles = (hn_ref.shape[1] - N_META) // ROW_TILE

    def body(i, carry):
        r = pl.multiple_of(i * ROW_TILE, ROW_TILE)
        lhs = hn_ref[0, pl.ds(r, ROW_TILE + N_META), :]
        lhs_out = lhs[N_META:, :]

        y = _dot(lhs, wp_ref[...])
        cv = _dot(lhs, wgc_ref[...]) * _dot(lhs, wv_ref[...])
        s = y + pltpu.roll(y, 1, 0)
        s = jnp.where(grp >= 1, s + pltpu.roll(s, 2, 0), s)
        s = jnp.where(grp >= 2, s + pltpu.roll(s, 4, 0), s)
        s = jnp.where(grp >= 3, s + pltpu.roll(s, 8, 0), s)
        pooled = (s * inv_window - y)[N_META:, :]

        conv = w0 * pltpu.roll(cv, 2, 0) + w1 * pltpu.roll(cv, 1, 0) + w2 * cv
        z = _dot(lhs_out, wgb_ref[...]) * conv[N_META:, :]
        z_ref[0, pl.ds(r, ROW_TILE), :] = z.astype(BF16)
        ga = _dot(lhs_out, wga_ref[...]) + bga_ref[...]
        gaya_ref[0, pl.ds(r, ROW_TILE), :] = (jax.nn.sigmoid(ga) * pooled).astype(BF16)
        gr = _dot(lhs_out, wgr_ref[...]) + bgr_ref[...]
        gs_ref[0, pl.ds(r, ROW_TILE), :] = jax.nn.sigmoid(gr).astype(BF16)
        return carry

    lax.fori_loop(0, n_tiles, body, 0)


def _inproj(hn, w_pool, w_in, b_gate, conv_w, conv_out_w, w_o):
    b, l, d = hn.shape
    seq = l - N_META
    cc = IN_CHUNK
    n_chunks = d // cc
    slab = d // (b * n_chunks)
    slab_spec = pl.BlockSpec((slab, d), lambda bi, c: (bi * n_chunks + c, 0))
    wspec = lambda first: pl.BlockSpec((d, cc), lambda bi, c: (0, first + c))
    bspec = lambda first: pl.BlockSpec((1, cc), lambda bi, c: (0, first + c))
    out_spec = pl.BlockSpec((1, seq, cc), lambda bi, c: (bi, 0, c))
    return pl.pallas_call(
        _inproj_kernel,
        grid=(b, n_chunks),
        in_specs=[pl.BlockSpec((1, l, d), lambda bi, c: (bi, 0, 0)),
                  wspec(0),
                  wspec(4 * n_chunks), wspec(1 * n_chunks), wspec(2 * n_chunks), wspec(3 * n_chunks),
                  wspec(5 * n_chunks),
                  bspec(0), bspec(n_chunks),
                  pl.BlockSpec((CONV_K, cc), lambda bi, c: (0, c)),
                  slab_spec, slab_spec],
        out_specs=[out_spec, out_spec, out_spec, slab_spec, slab_spec],
        out_shape=[jax.ShapeDtypeStruct((b, seq, d), BF16)] * 3 + [jax.ShapeDtypeStruct((d, d), BF16)] * 2,
        compiler_params=_params(60, ("arbitrary", "arbitrary")),
        name="inproj_mixers",
    )(hn, w_pool, w_in, w_in, w_in, w_in, w_in, b_gate, b_gate, conv_w, conv_out_w, w_o)


def _mix_kernel(z_ref, gaya_ref, gs_ref, x_ref, cw_ref, wo_ref, g_ref, h1_ref, hn2_ref):
    yb = _dot(z_ref[...], cw_ref[...])
    mix = gaya_ref[...].astype(F32) + gs_ref[...].astype(F32) * yb
    h1 = x_ref[...] + _dot(mix.astype(BF16), wo_ref[...])
    h1_ref[...] = h1
    hn2_ref[...] = _rms(h1, g_ref[...]).astype(BF16)


def _mix(z, gaya, gs, x2d, conv_out_w, w_o, g):
    m, d = x2d.shape
    rows = pl.BlockSpec((MIX_ROWS, d), lambda i: (i, 0))
    whole = pl.BlockSpec((d, d), lambda i: (0, 0), pipeline_mode=pl.Buffered(1))
    return pl.pallas_call(
        _mix_kernel,
        grid=(m // MIX_ROWS,),
        in_specs=[rows, rows, rows, rows, whole, whole, pl.BlockSpec((1, d), lambda i: (0, 0))],
        out_specs=[rows, rows],
        out_shape=[jax.ShapeDtypeStruct((m, d), F32), jax.ShapeDtypeStruct((m, d), BF16)],
        compiler_params=_params(56, ("arbitrary",)),
        name="mix_out_proj",
    )(z, gaya, gs, x2d, conv_out_w, w_o, g)


def _ffn_kernel(wg_ref, wu_ref, wd_ref, g_ref, hn_hbm, h1_hbm, out_hbm,
                hn_s, acc, stage, sem_hn, sem_h1, sem_out):
    bi = pl.program_id(0)
    j = pl.program_id(1)
    last_b = pl.num_programs(0) - 1
    last_j = pl.num_programs(1) - 1
    seq, d = acc.shape
    n_tiles = seq // ROW_TILE
    n_fin = seq // FINISH_ROWS
    fin_per_tile = ROW_TILE // FINISH_ROWS

    def hn_copy(batch, t):
        rows = pl.ds(t * ROW_TILE, ROW_TILE)
        return pltpu.make_async_copy(hn_hbm.at[batch, rows, :], hn_s.at[rows, :], sem_hn.at[t])

    def h1_copy(s):
        rows = pl.ds(s * FINISH_ROWS, FINISH_ROWS)
        return pltpu.make_async_copy(h1_hbm.at[bi, rows, :], stage.at[s % 2], sem_h1.at[s % 2])

    def out_copy(batch, t):
        rows = pl.ds(t * ROW_TILE, ROW_TILE)
        return pltpu.make_async_copy(acc.at[rows, :], out_hbm.at[batch, rows, :], sem_out.at[t])

    @pl.when((bi == 0) & (j == 0))
    def _():
        for t in range(n_tiles):
            hn_copy(0, t).start()
        acc[...] = jnp.zeros_like(acc)

    @pl.when(j < n_fin)
    def _():
        h1_copy(j).start()

    for t in range(n_tiles):
        rows = pl.ds(t * ROW_TILE, ROW_TILE)

        @pl.when(j == 0)
        def _():
            hn_copy(bi, t).wait()

            @pl.when(bi > 0)
            def _():
                out_copy(bi - 1, t).wait()

        lhs = hn_s[rows, :]
        gate = _dot(lhs, wg_ref[...])
        up = _dot(lhs, wu_ref[...])
        act = (jax.nn.silu(gate) * up).astype(BF16)
        for n in range(d // DOWN_COLS):
            cols = slice(n * DOWN_COLS, (n + 1) * DOWN_COLS)
            prev = jnp.where(j == 0, 0.0, acc[rows, cols])
            acc[rows, cols] = prev + _dot(act, wd_ref[:, cols])

        @pl.when((j == last_j) & (bi < last_b))
        def _():
            hn_copy(bi + 1, t).start()

    @pl.when((j >= 1) & (j <= n_fin))
    def _():
        piece = j - 1
        h1_copy(piece).wait()
        rows = pl.ds(pl.multiple_of(piece * FINISH_ROWS, FINISH_ROWS), FINISH_ROWS)
        acc[rows, :] += stage[piece % 2]

    @pl.when(j == last_j)
    def _():
        for s in range(n_fin):
            rows = pl.ds(s * FINISH_ROWS, FINISH_ROWS)
            acc[rows, :] = _rms(acc[rows, :], g_ref[...])
            if (s + 1) % fin_per_tile == 0:
                out_copy(bi, s // fin_per_tile).start()

        @pl.when(bi == last_b)
        def _():
            for t in range(n_tiles):
                out_copy(bi, t).wait()


def _ffn(hn2, h1, w_gate_up, w_down, g):
    b, seq, d = hn2.shape
    fc = FFN_CHUNK
    n_chunks = FFN_HIDDEN // fc
    n_tiles = seq // ROW_TILE
    assert n_chunks > seq // FINISH_ROWS
    return pl.pallas_call(
        _ffn_kernel,
        grid=(b, n_chunks),
        in_specs=[pl.BlockSpec((d, fc), lambda bi, j: (0, j)),
                  pl.BlockSpec((d, fc), lambda bi, j: (0, n_chunks + j)),
                  pl.BlockSpec((fc, d), lambda bi, j: (j, 0)),
                  pl.BlockSpec((1, d), lambda bi, j: (0, 0)),
                  pl.BlockSpec(memory_space=pl.ANY),
                  pl.BlockSpec(memory_space=pl.ANY)],
        out_specs=pl.BlockSpec(memory_space=pl.ANY),
        out_shape=jax.ShapeDtypeStruct((b, seq, d), F32),
        scratch_shapes=[pltpu.VMEM((seq, d), BF16),
                        pltpu.VMEM((seq, d), F32),
                        pltpu.VMEM((2, FINISH_ROWS, d), F32),
                        pltpu.SemaphoreType.DMA((n_tiles,)),
                        pltpu.SemaphoreType.DMA((2,)),
                        pltpu.SemaphoreType.DMA((n_tiles,))],
        compiler_params=_params(60, ("arbitrary", "arbitrary")),
        name="ffn",
    )(w_gate_up, w_gate_up, w_down, g, hn2, h1)


def kernel(x, meta_tokens, norm_mix_g, w_in, b_gate, pool_w, pool_scale, conv_w, conv_out_w, w_o,
           norm_ffn_g, w_gate_up, w_down, norm_final_g):
    b, seq, d = x.shape
    row = lambda v: v.reshape(1, -1)
    hn = _norm_in(x, meta_tokens, row(norm_mix_g))
    w_pool = _fold_pool(w_in, pool_w, row(pool_scale))
    gaya, z, gs, conv_out_bf, w_o_bf = _inproj(hn, w_pool, w_in, row(b_gate), conv_w, conv_out_w, w_o)
    m = b * seq
    h1, hn2 = _mix(z.reshape(m, d), gaya.reshape(m, d), gs.reshape(m, d), x.reshape(m, d),
                   conv_out_bf, w_o_bf, row(norm_ffn_g))
    return _ffn(hn2.reshape(b, seq, d), h1.reshape(b, seq, d), w_gate_up, w_down, row(norm_final_g))
```

```python
import jax
import jax.numpy as jnp
from jax import lax
from jax.experimental import pallas as pl
from jax.experimental.pallas import tpu as pltpu

D_MODEL = 2048
N_META = 16
POOL_WINDOWS = (2, 4, 8, 16)
POOL_GROUP = D_MODEL // len(POOL_WINDOWS)
CONV_K = 3
FFN_HIDDEN = ((8 * D_MODEL // 3 + 255) // 256) * 256
EPS = 1e-6

F32 = jnp.float32
BF16 = jnp.bfloat16

MIB = 1024 * 1024
ROW_TILE = 1024
NORM_ROWS = 256
IN_CHUNK = 256
FFN_CHUNK = 512
MIX_ROWS = 512
DOWN_COLS = 512
FINISH_ROWS = 256

assert N_META >= max(POOL_WINDOWS) and N_META >= CONV_K
assert POOL_WINDOWS == (2, 4, 8, 16)


def _dot(a, b):
    return lax.dot_general(a, b, (((1,), (0,)), ((), ())), preferred_element_type=F32)


def _rms(xf, g):
    ms = jnp.mean(xf * xf, axis=-1, keepdims=True)
    return xf * lax.rsqrt(ms + EPS) * g


def _params(vmem_mib, semantics):
    return pltpu.CompilerParams(dimension_semantics=semantics,
                                vmem_limit_bytes=vmem_mib * MIB)


def _fold_pool_kernel(wu_ref, pw_ref, ps_ref, o_ref):
    o_ref[...] = (_dot(wu_ref[...], pw_ref[0]) * ps_ref[...]).astype(BF16)


def _fold_pool(w_in, pool_w, pool_scale):
    d = w_in.shape[0]
    pg = POOL_GROUP
    return pl.pallas_call(
        _fold_pool_kernel,
        grid=(len(POOL_WINDOWS),),
        in_specs=[pl.BlockSpec((d, pg), lambda g: (0, g)),
                  pl.BlockSpec((1, pg, pg), lambda g: (g, 0, 0)),
                  pl.BlockSpec((1, pg), lambda g: (0, g))],
        out_specs=pl.BlockSpec((d, pg), lambda g: (0, g)),
        out_shape=jax.ShapeDtypeStruct((d, d), BF16),
        compiler_params=_params(32, ("arbitrary",)),
        name="fold_pool",
    )(w_in, pool_w, pool_scale)


def _inproj_kernel(meta_ref, g_ref, wp_ref, wga_ref, wgb_ref, wgc_ref, wv_ref, wgr_ref, bga_ref, bgr_ref,
                   cw_ref, cow_ref, wo_ref, x_hbm, gaya_ref, z_ref, gs_ref, cow_bf_ref, wo_bf_ref,
                   hn_s, stage, sem_x):
    bi = pl.program_id(0)
    chunk = pl.program_id(1)
    last_b = pl.num_programs(0) - 1
    n_pieces = (hn_s.shape[1] - N_META) // NORM_ROWS
    slot = bi % 2
    g = g_ref[...]
    cow_bf_ref[...] = cow_ref[...].astype(BF16)
    wo_bf_ref[...] = wo_ref[...].astype(BF16)

    def x_copy(batch, piece, buf):
        rows = pl.ds(piece * NORM_ROWS, NORM_ROWS)
        return pltpu.make_async_copy(x_hbm.at[batch, rows, :], stage.at[buf], sem_x.at[buf])

    def norm_piece(dst_slot, piece, buf):
        rows = pl.ds(pl.multiple_of(N_META + piece * NORM_ROWS, N_META), NORM_ROWS)
        hn_s[dst_slot, rows, :] = _rms(stage[buf], g).astype(BF16)

    @pl.when((bi == 0) & (chunk == 0))
    def _():
        meta_n = _rms(meta_ref[...], g).astype(BF16)
        hn_s[0, 0:N_META, :] = meta_n
        hn_s[1, 0:N_META, :] = meta_n
        x_copy(0, 0, 0).start()
        for piece in range(n_pieces):
            if piece + 1 < n_pieces:
                x_copy(0, piece + 1, (piece + 1) % 2).start()
            x_copy(0, piece, piece % 2).wait()
            norm_piece(0, piece, piece % 2)

    @pl.when(bi < last_b)
    def _():
        x_copy(bi + 1, chunk, 0).start()

    grp = chunk // (POOL_GROUP // IN_CHUNK)
    w0 = cw_ref[0:1, :]
    w1 = cw_ref[1:2, :]
    w2 = cw_ref[2:3, :]
    inv_window = jnp.where(grp == 0, 0.5, jnp.where(grp == 1, 0.25, jnp.where(grp == 2, 0.125, 0.0625)))
    n_tiles = (hn_s.shape[1] - N_META) // ROW_TILE

    def body(i, carry):
        r = pl.multiple_of(i * ROW_TILE, ROW_TILE)
        lhs = hn_s[slot, pl.ds(r, ROW_TILE + N_META), :]
        lhs_out = lhs[N_META:, :]

        y = _dot(lhs, wp_ref[...])
        cv = _dot(lhs, wgc_ref[...]) * _dot(lhs, wv_ref[...])
        s = y + pltpu.roll(y, 1, 0)
        s = jnp.where(grp >= 1, s + pltpu.roll(s, 2, 0), s)
        s = jnp.where(grp >= 2, s + pltpu.roll(s, 4, 0), s)
        s = jnp.where(grp >= 3, s + pltpu.roll(s, 8, 0), s)
        pooled = (s * inv_window - y)[N_META:, :]

        conv = w0 * pltpu.roll(cv, 2, 0) + w1 * pltpu.roll(cv, 1, 0) + w2 * cv
        z = _dot(lhs_out, wgb_ref[...]) * conv[N_META:, :]
        z_ref[0, pl.ds(r, ROW_TILE), :] = z.astype(BF16)
        ga = _dot(lhs_out, wga_ref[...]) + bga_ref[...]
        gaya_ref[0, pl.ds(r, ROW_TILE), :] = (jax.nn.sigmoid(ga) * pooled).astype(BF16)
        gr = _dot(lhs_out, wgr_ref[...]) + bgr_ref[...]
        gs_ref[0, pl.ds(r, ROW_TILE), :] = jax.nn.sigmoid(gr).astype(BF16)
        return carry

    lax.fori_loop(0, n_tiles, body, 0)

    @pl.when(bi < last_b)
    def _():
        x_copy(bi + 1, chunk, 0).wait()
        norm_piece(1 - slot, chunk, 0)


def _inproj(x, meta, g, w_pool, w_in, b_gate, conv_w, conv_out_w, w_o):
    b, seq, d = x.shape
    cc = IN_CHUNK
    n_chunks = d // cc
    assert seq == n_chunks * NORM_ROWS
    slab = d // (b * n_chunks)
    slab_spec = pl.BlockSpec((slab, d), lambda bi, c: (bi * n_chunks + c, 0))
    wspec = lambda first: pl.BlockSpec((d, cc), lambda bi, c: (0, first + c))
    bspec = lambda first: pl.BlockSpec((1, cc), lambda bi, c: (0, first + c))
    out_spec = pl.BlockSpec((1, seq, cc), lambda bi, c: (bi, 0, c))
    return pl.pallas_call(
        _inproj_kernel,
        grid=(b, n_chunks),
        in_specs=[pl.BlockSpec((N_META, d), lambda bi, c: (0, 0)),
                  pl.BlockSpec((1, d), lambda bi, c: (0, 0)),
                  wspec(0),
                  wspec(4 * n_chunks), wspec(1 * n_chunks), wspec(2 * n_chunks), wspec(3 * n_chunks),
                  wspec(5 * n_chunks),
                  bspec(0), bspec(n_chunks),
                  pl.BlockSpec((CONV_K, cc), lambda bi, c: (0, c)),
                  slab_spec, slab_spec,
                  pl.BlockSpec(memory_space=pl.ANY)],
        out_specs=[out_spec, out_spec, out_spec, slab_spec, slab_spec],
        out_shape=[jax.ShapeDtypeStruct((b, seq, d), BF16)] * 3 + [jax.ShapeDtypeStruct((d, d), BF16)] * 2,
        scratch_shapes=[pltpu.VMEM((2, seq + N_META, d), BF16),
                        pltpu.VMEM((2, NORM_ROWS, d), F32),
                        pltpu.SemaphoreType.DMA((2,))],
        compiler_params=_params(60, ("arbitrary", "arbitrary")),
        name="inproj_mixers",
    )(meta, g, w_pool, w_in, w_in, w_in, w_in, w_in, b_gate, b_gate, conv_w, conv_out_w, w_o, x)


def _mix_kernel(z_ref, gaya_ref, gs_ref, x_ref, cw_ref, wo_ref, g_ref, h1_ref, hn2_ref):
    yb = _dot(z_ref[...], cw_ref[...])
    mix = gaya_ref[...].astype(F32) + gs_ref[...].astype(F32) * yb
    h1 = x_ref[...] + _dot(mix.astype(BF16), wo_ref[...])
    h1_ref[...] = h1
    hn2_ref[...] = _rms(h1, g_ref[...]).astype(BF16)


def _mix(z, gaya, gs, x2d, conv_out_w, w_o, g):
    m, d = x2d.shape
    rows = pl.BlockSpec((MIX_ROWS, d), lambda i: (i, 0))
    whole = pl.BlockSpec((d, d), lambda i: (0, 0), pipeline_mode=pl.Buffered(1))
    return pl.pallas_call(
        _mix_kernel,
        grid=(m // MIX_ROWS,),
        in_specs=[rows, rows, rows, rows, whole, whole, pl.BlockSpec((1, d), lambda i: (0, 0))],
        out_specs=[rows, rows],
        out_shape=[jax.ShapeDtypeStruct((m, d), F32), jax.ShapeDtypeStruct((m, d), BF16)],
        compiler_params=_params(56, ("arbitrary",)),
        name="mix_out_proj",
    )(z, gaya, gs, x2d, conv_out_w, w_o, g)


def _ffn_kernel(wg_ref, wu_ref, wd_ref, g_ref, hn_hbm, h1_hbm, out_hbm,
                hn_s, acc, stage, sem_hn, sem_h1, sem_out):
    bi = pl.program_id(0)
    j = pl.program_id(1)
    last_b = pl.num_programs(0) - 1
    last_j = pl.num_programs(1) - 1
    seq, d = acc.shape
    n_tiles = seq // ROW_TILE
    n_fin = seq // FINISH_ROWS
    fin_per_tile = ROW_TILE // FINISH_ROWS

    def hn_copy(batch, t):
        rows = pl.ds(t * ROW_TILE, ROW_TILE)
        return pltpu.make_async_copy(hn_hbm.at[batch, rows, :], hn_s.at[rows, :], sem_hn.at[t])

    def h1_copy(s):
        rows = pl.ds(s * FINISH_ROWS, FINISH_ROWS)
        return pltpu.make_async_copy(h1_hbm.at[bi, rows, :], stage.at[s % 2], sem_h1.at[s % 2])

    def out_copy(batch, t):
        rows = pl.ds(t * ROW_TILE, ROW_TILE)
        return pltpu.make_async_copy(acc.at[rows, :], out_hbm.at[batch, rows, :], sem_out.at[t])

    @pl.when((bi == 0) & (j == 0))
    def _():
        for t in range(n_tiles):
            hn_copy(0, t).start()
        acc[...] = jnp.zeros_like(acc)

    @pl.when(j < n_fin)
    def _():
        h1_copy(j).start()

    for t in range(n_tiles):
        rows = pl.ds(t * ROW_TILE, ROW_TILE)

        @pl.when(j == 0)
        def _():
            hn_copy(bi, t).wait()

            @pl.when(bi > 0)
            def _():
                out_copy(bi - 1, t).wait()

        lhs = hn_s[rows, :]
        gate = _dot(lhs, wg_ref[...])
        up = _dot(lhs, wu_ref[...])
        act = (jax.nn.silu(gate) * up).astype(BF16)
        for n in range(d // DOWN_COLS):
            cols = slice(n * DOWN_COLS, (n + 1) * DOWN_COLS)
            prev = jnp.where(j == 0, 0.0, acc[rows, cols])
            acc[rows, cols] = prev + _dot(act, wd_ref[:, cols])

        @pl.when((j == last_j) & (bi < last_b))
        def _():
            hn_copy(bi + 1, t).start()

    @pl.when((j >= 1) & (j <= n_fin))
    def _():
        piece = j - 1
        h1_copy(piece).wait()
        rows = pl.ds(pl.multiple_of(piece * FINISH_ROWS, FINISH_ROWS), FINISH_ROWS)
        acc[rows, :] += stage[piece % 2]

    @pl.when(j == last_j)
    def _():
        for s in range(n_fin):
            rows = pl.ds(s * FINISH_ROWS, FINISH_ROWS)
            acc[rows, :] = _rms(acc[rows, :], g_ref[...])
            if (s + 1) % fin_per_tile == 0:
                out_copy(bi, s // fin_per_tile).start()

        @pl.when(bi == last_b)
        def _():
            for t in range(n_tiles):
                out_copy(bi, t).wait()


def _ffn(hn2, h1, w_gate_up, w_down, g):
    b, seq, d = hn2.shape
    fc = FFN_CHUNK
    n_chunks = FFN_HIDDEN // fc
    n_tiles = seq // ROW_TILE
    assert n_chunks > seq // FINISH_ROWS
    return pl.pallas_call(
        _ffn_kernel,
        grid=(b, n_chunks),
        in_specs=[pl.BlockSpec((d, fc), lambda bi, j: (0, j)),
                  pl.BlockSpec((d, fc), lambda bi, j: (0, n_chunks + j)),
                  pl.BlockSpec((fc, d), lambda bi, j: (j, 0)),
                  pl.BlockSpec((1, d), lambda bi, j: (0, 0)),
                  pl.BlockSpec(memory_space=pl.ANY),
                  pl.BlockSpec(memory_space=pl.ANY)],
        out_specs=pl.BlockSpec(memory_space=pl.ANY),
        out_shape=jax.ShapeDtypeStruct((b, seq, d), F32),
        scratch_shapes=[pltpu.VMEM((seq, d), BF16),
                        pltpu.VMEM((seq, d), F32),
                        pltpu.VMEM((2, FINISH_ROWS, d), F32),
                        pltpu.SemaphoreType.DMA((n_tiles,)),
                        pltpu.SemaphoreType.DMA((2,)),
                        pltpu.SemaphoreType.DMA((n_tiles,))],
        compiler_params=_params(60, ("arbitrary", "arbitrary")),
        name="ffn",
    )(w_gate_up, w_gate_up, w_down, g, hn2, h1)


def kernel(x, meta_tokens, norm_mix_g, w_in, b_gate, pool_w, pool_scale, conv_w, conv_out_w, w_o,
           norm_ffn_g, w_gate_up, w_down, norm_final_g):
    b, seq, d = x.shape
    row = lambda v: v.reshape(1, -1)
    w_pool = _fold_pool(w_in, pool_w, row(pool_scale))
    gaya, z, gs, conv_out_bf, w_o_bf = _inproj(x, meta_tokens, row(norm_mix_g), w_pool, w_in, row(b_gate),
                                               conv_w, conv_out_w, w_o)
    m = b * seq
    h1, hn2 = _mix(z.reshape(m, d), gaya.reshape(m, d), gs.reshape(m, d), x.reshape(m, d),
                   conv_out_bf, w_o_bf, row(norm_ffn_g))
    return _ffn(hn2.reshape(b, seq, d), h1.reshape(b, seq, d), w_gate_up, w_down, row(norm_final_g))
```

```python
import jax
import jax.numpy as jnp
from jax import lax
from jax.experimental import pallas as pl
from jax.experimental.pallas import tpu as pltpu

D_MODEL = 2048
N_META = 16
POOL_WINDOWS = (2, 4, 8, 16)
POOL_GROUP = D_MODEL // len(POOL_WINDOWS)
CONV_K = 3
FFN_HIDDEN = ((8 * D_MODEL // 3 + 255) // 256) * 256
EPS = 1e-6

F32 = jnp.float32
BF16 = jnp.bfloat16

MIB = 1024 * 1024
ROW_TILE = 1024
NORM_ROWS = 256
IN_CHUNK = 256
FFN_CHUNK = 512
MIX_ROWS = 512
DOWN_COLS = 512
FINISH_ROWS = 256
OUT_ROWS = 512

assert N_META >= max(POOL_WINDOWS) and N_META >= CONV_K
assert POOL_WINDOWS == (2, 4, 8, 16)


def _dot(a, b):
    return lax.dot_general(a, b, (((1,), (0,)), ((), ())), preferred_element_type=F32)


def _rms(xf, g):
    ms = jnp.mean(xf * xf, axis=-1, keepdims=True)
    return xf * lax.rsqrt(ms + EPS) * g


def _params(vmem_mib, semantics):
    return pltpu.CompilerParams(dimension_semantics=semantics,
                                vmem_limit_bytes=vmem_mib * MIB)


def _fold_pool_kernel(wu_ref, pw_ref, ps_ref, o_ref):
    o_ref[...] = (_dot(wu_ref[...], pw_ref[0]) * ps_ref[...]).astype(BF16)


def _fold_pool(w_in, pool_w, pool_scale):
    d = w_in.shape[0]
    pg = POOL_GROUP
    return pl.pallas_call(
        _fold_pool_kernel,
        grid=(len(POOL_WINDOWS),),
        in_specs=[pl.BlockSpec((d, pg), lambda g: (0, g)),
                  pl.BlockSpec((1, pg, pg), lambda g: (g, 0, 0)),
                  pl.BlockSpec((1, pg), lambda g: (0, g))],
        out_specs=pl.BlockSpec((d, pg), lambda g: (0, g)),
        out_shape=jax.ShapeDtypeStruct((d, d), BF16),
        compiler_params=_params(32, ("arbitrary",)),
        name="fold_pool",
    )(w_in, pool_w, pool_scale)


def _inproj_kernel(meta_ref, g_ref, wp_ref, wga_ref, wgb_ref, wgc_ref, wv_ref, wgr_ref, bga_ref, bgr_ref,
                   cw_ref, cow_ref, wo_ref, x_hbm, gaya_ref, z_ref, gs_ref, cow_bf_ref, wo_bf_ref,
                   hn_s, stage, sem_x):
    bi = pl.program_id(0)
    chunk = pl.program_id(1)
    last_b = pl.num_programs(0) - 1
    n_pieces = (hn_s.shape[1] - N_META) // NORM_ROWS
    slot = bi % 2
    g = g_ref[...]
    cow_bf_ref[...] = cow_ref[...].astype(BF16)
    wo_bf_ref[...] = wo_ref[...].astype(BF16)

    def x_copy(batch, piece, buf):
        rows = pl.ds(piece * NORM_ROWS, NORM_ROWS)
        return pltpu.make_async_copy(x_hbm.at[batch, rows, :], stage.at[buf], sem_x.at[buf])

    def norm_piece(dst_slot, piece, buf):
        rows = pl.ds(pl.multiple_of(N_META + piece * NORM_ROWS, N_META), NORM_ROWS)
        hn_s[dst_slot, rows, :] = _rms(stage[buf], g).astype(BF16)

    @pl.when((bi == 0) & (chunk == 0))
    def _():
        meta_n = _rms(meta_ref[...], g).astype(BF16)
        hn_s[0, 0:N_META, :] = meta_n
        hn_s[1, 0:N_META, :] = meta_n
        x_copy(0, 0, 0).start()
        for piece in range(n_pieces):
            if piece + 1 < n_pieces:
                x_copy(0, piece + 1, (piece + 1) % 2).start()
            x_copy(0, piece, piece % 2).wait()
            norm_piece(0, piece, piece % 2)

    @pl.when(bi < last_b)
    def _():
        x_copy(bi + 1, chunk, 0).start()

    grp = chunk // (POOL_GROUP // IN_CHUNK)
    w0 = cw_ref[0:1, :]
    w1 = cw_ref[1:2, :]
    w2 = cw_ref[2:3, :]
    inv_window = jnp.where(grp == 0, 0.5, jnp.where(grp == 1, 0.25, jnp.where(grp == 2, 0.125, 0.0625)))
    n_tiles = (hn_s.shape[1] - N_META) // ROW_TILE

    def body(i, carry):
        r = pl.multiple_of(i * ROW_TILE, ROW_TILE)
        lhs = hn_s[slot, pl.ds(r, ROW_TILE + N_META), :]
        lhs_out = lhs[N_META:, :]

        y = _dot(lhs, wp_ref[...])
        cv = _dot(lhs, wgc_ref[...]) * _dot(lhs, wv_ref[...])
        s = y + pltpu.roll(y, 1, 0)
        s = jnp.where(grp >= 1, s + pltpu.roll(s, 2, 0), s)
        s = jnp.where(grp >= 2, s + pltpu.roll(s, 4, 0), s)
        s = jnp.where(grp >= 3, s + pltpu.roll(s, 8, 0), s)
        pooled = (s * inv_window - y)[N_META:, :]

        conv = w0 * pltpu.roll(cv, 2, 0) + w1 * pltpu.roll(cv, 1, 0) + w2 * cv
        z = _dot(lhs_out, wgb_ref[...]) * conv[N_META:, :]
        z_ref[0, pl.ds(r, ROW_TILE), :] = z.astype(BF16)
        ga = _dot(lhs_out, wga_ref[...]) + bga_ref[...]
        gaya_ref[0, pl.ds(r, ROW_TILE), :] = (jax.nn.sigmoid(ga) * pooled).astype(BF16)
        gr = _dot(lhs_out, wgr_ref[...]) + bgr_ref[...]
        gs_ref[0, pl.ds(r, ROW_TILE), :] = jax.nn.sigmoid(gr).astype(BF16)
        return carry

    for i in range(n_tiles):
        body(i, 0)

    @pl.when(bi < last_b)
    def _():
        x_copy(bi + 1, chunk, 0).wait()
        norm_piece(1 - slot, chunk, 0)


def _inproj(x, meta, g, w_pool, w_in, b_gate, conv_w, conv_out_w, w_o):
    b, seq, d = x.shape
    cc = IN_CHUNK
    n_chunks = d // cc
    assert seq == n_chunks * NORM_ROWS
    slab = d // (b * n_chunks)
    slab_spec = pl.BlockSpec((slab, d), lambda bi, c: (bi * n_chunks + c, 0))
    wspec = lambda first: pl.BlockSpec((d, cc), lambda bi, c: (0, first + c))
    bspec = lambda first: pl.BlockSpec((1, cc), lambda bi, c: (0, first + c))
    out_spec = pl.BlockSpec((1, seq, cc), lambda bi, c: (bi, 0, c))
    return pl.pallas_call(
        _inproj_kernel,
        grid=(b, n_chunks),
        in_specs=[pl.BlockSpec((N_META, d), lambda bi, c: (0, 0)),
                  pl.BlockSpec((1, d), lambda bi, c: (0, 0)),
                  wspec(0),
                  wspec(4 * n_chunks), wspec(1 * n_chunks), wspec(2 * n_chunks), wspec(3 * n_chunks),
                  wspec(5 * n_chunks),
                  bspec(0), bspec(n_chunks),
                  pl.BlockSpec((CONV_K, cc), lambda bi, c: (0, c)),
                  slab_spec, slab_spec,
                  pl.BlockSpec(memory_space=pl.ANY)],
        out_specs=[out_spec, out_spec, out_spec, slab_spec, slab_spec],
        out_shape=[jax.ShapeDtypeStruct((b, seq, d), BF16)] * 3 + [jax.ShapeDtypeStruct((d, d), BF16)] * 2,
        scratch_shapes=[pltpu.VMEM((2, seq + N_META, d), BF16),
                        pltpu.VMEM((2, NORM_ROWS, d), F32),
                        pltpu.SemaphoreType.DMA((2,))],
        compiler_params=_params(60, ("arbitrary", "arbitrary")),
        name="inproj_mixers",
    )(meta, g, w_pool, w_in, w_in, w_in, w_in, w_in, b_gate, b_gate, conv_w, conv_out_w, w_o, x)


def _mix_kernel(z_ref, gaya_ref, gs_ref, x_ref, cw_ref, wo_ref, g_ref, h1_ref, hn2_ref):
    yb = _dot(z_ref[...], cw_ref[...])
    mix = gaya_ref[...].astype(F32) + gs_ref[...].astype(F32) * yb
    h1 = x_ref[...] + _dot(mix.astype(BF16), wo_ref[...])
    h1_ref[...] = h1
    hn2_ref[...] = _rms(h1, g_ref[...]).astype(BF16)


def _mix(z, gaya, gs, x2d, conv_out_w, w_o, g):
    m, d = x2d.shape
    rows = pl.BlockSpec((MIX_ROWS, d), lambda i: (i, 0))
    whole = pl.BlockSpec((d, d), lambda i: (0, 0), pipeline_mode=pl.Buffered(1))
    return pl.pallas_call(
        _mix_kernel,
        grid=(m // MIX_ROWS,),
        in_specs=[rows, rows, rows, rows, whole, whole, pl.BlockSpec((1, d), lambda i: (0, 0))],
        out_specs=[rows, rows],
        out_shape=[jax.ShapeDtypeStruct((m, d), F32), jax.ShapeDtypeStruct((m, d), BF16)],
        compiler_params=_params(56, ("arbitrary",)),
        name="mix_out_proj",
    )(z, gaya, gs, x2d, conv_out_w, w_o, g)


def _ffn_kernel(wg_ref, wu_ref, wd_ref, g_ref, hn_hbm, h1_hbm, out_hbm,
                hn_s, acc, stage, sem_hn, sem_h1, sem_out):
    bi = pl.program_id(0)
    j = pl.program_id(1)
    last_b = pl.num_programs(0) - 1
    last_j = pl.num_programs(1) - 1
    seq, d = acc.shape
    n_tiles = seq // ROW_TILE
    n_fin = seq // FINISH_ROWS
    n_out = seq // OUT_ROWS
    fin_per_out = OUT_ROWS // FINISH_ROWS

    def hn_copy(batch, t):
        rows = pl.ds(t * ROW_TILE, ROW_TILE)
        return pltpu.make_async_copy(hn_hbm.at[batch, rows, :], hn_s.at[rows, :], sem_hn.at[t])

    def h1_copy(s):
        rows = pl.ds(s * FINISH_ROWS, FINISH_ROWS)
        return pltpu.make_async_copy(h1_hbm.at[bi, rows, :], stage.at[s % 2], sem_h1.at[s % 2])

    def out_copy(batch, p):
        rows = pl.ds(p * OUT_ROWS, OUT_ROWS)
        return pltpu.make_async_copy(acc.at[rows, :], out_hbm.at[batch, rows, :], sem_out.at[p])

    @pl.when((bi == 0) & (j == 0))
    def _():
        for t in range(n_tiles):
            hn_copy(0, t).start()
        acc[...] = jnp.zeros_like(acc)

    @pl.when(j == 0)
    def _():
        for t in range(n_tiles):
            hn_copy(bi, t).wait()

        @pl.when(bi > 0)
        def _():
            for p in range(n_out):
                out_copy(bi - 1, p).wait()

    @pl.when(j < n_fin)
    def _():
        h1_copy(j).start()

    for t in range(n_tiles):
        rows = pl.ds(t * ROW_TILE, ROW_TILE)
        lhs = hn_s[rows, :]
        gate = _dot(lhs, wg_ref[...])
        up = _dot(lhs, wu_ref[...])
        act = (jax.nn.silu(gate) * up).astype(BF16)
        for n in range(d // DOWN_COLS):
            cols = slice(n * DOWN_COLS, (n + 1) * DOWN_COLS)
            prev = jnp.where(j == 0, 0.0, acc[rows, cols])
            acc[rows, cols] = prev + _dot(act, wd_ref[:, cols])

    @pl.when((j == last_j) & (bi < last_b))
    def _():
        for t in range(n_tiles):
            hn_copy(bi + 1, t).start()

    @pl.when((j >= 1) & (j <= n_fin))
    def _():
        piece = j - 1
        h1_copy(piece).wait()
        rows = pl.ds(pl.multiple_of(piece * FINISH_ROWS, FINISH_ROWS), FINISH_ROWS)
        acc[rows, :] += stage[piece % 2]

    @pl.when(j == last_j)
    def _():
        for s in range(n_fin):
            rows = pl.ds(s * FINISH_ROWS, FINISH_ROWS)
            acc[rows, :] = _rms(acc[rows, :], g_ref[...])
            if (s + 1) % fin_per_out == 0:
                out_copy(bi, s // fin_per_out).start()

        @pl.when(bi == last_b)
        def _():
            for p in range(n_out):
                out_copy(bi, p).wait()


def _ffn(hn2, h1, w_gate_up, w_down, g):
    b, seq, d = hn2.shape
    fc = FFN_CHUNK
    n_chunks = FFN_HIDDEN // fc
    n_tiles = seq // ROW_TILE
    assert n_chunks > seq // FINISH_ROWS
    return pl.pallas_call(
        _ffn_kernel,
        grid=(b, n_chunks),
        in_specs=[pl.BlockSpec((d, fc), lambda bi, j: (0, j)),
                  pl.BlockSpec((d, fc), lambda bi, j: (0, n_chunks + j)),
                  pl.BlockSpec((fc, d), lambda bi, j: (j, 0)),
                  pl.BlockSpec((1, d), lambda bi, j: (0, 0)),
                  pl.BlockSpec(memory_space=pl.ANY),
                  pl.BlockSpec(memory_space=pl.ANY)],
        out_specs=pl.BlockSpec(memory_space=pl.ANY),
        out_shape=jax.ShapeDtypeStruct((b, seq, d), F32),
        scratch_shapes=[pltpu.VMEM((seq, d), BF16),
                        pltpu.VMEM((seq, d), F32),
                        pltpu.VMEM((2, FINISH_ROWS, d), F32),
                        pltpu.SemaphoreType.DMA((n_tiles,)),
                        pltpu.SemaphoreType.DMA((2,)),
                        pltpu.SemaphoreType.DMA((seq // OUT_ROWS,))],
        compiler_params=_params(60, ("arbitrary", "arbitrary")),
        name="ffn",
    )(w_gate_up, w_gate_up, w_down, g, hn2, h1)


def kernel(x, meta_tokens, norm_mix_g, w_in, b_gate, pool_w, pool_scale, conv_w, conv_out_w, w_o,
           norm_ffn_g, w_gate_up, w_down, norm_final_g):
    b, seq, d = x.shape
    row = lambda v: v.reshape(1, -1)
    w_pool = _fold_pool(w_in, pool_w, row(pool_scale))
    gaya, z, gs, conv_out_bf, w_o_bf = _inproj(x, meta_tokens, row(norm_mix_g), w_pool, w_in, row(b_gate),
                                               conv_w, conv_out_w, w_o)
    m = b * seq
    h1, hn2 = _mix(z.reshape(m, d), gaya.reshape(m, d), gs.reshape(m, d), x.reshape(m, d),
                   conv_out_bf, w_o_bf, row(norm_ffn_g))
    return _ffn(hn2.reshape(b, seq, d), h1.reshape(b, seq, d), w_gate_up, w_down, row(norm_final_g))
```

```python
import jax
import jax.numpy as jnp
from jax import lax
from jax.experimental import pallas as pl
from jax.experimental.pallas import tpu as pltpu

D_MODEL = 2048
N_META = 16
POOL_WINDOWS = (2, 4, 8, 16)
POOL_GROUP = D_MODEL // len(POOL_WINDOWS)
CONV_K = 3
FFN_HIDDEN = ((8 * D_MODEL // 3 + 255) // 256) * 256
EPS = 1e-6

F32 = jnp.float32
BF16 = jnp.bfloat16

MIB = 1024 * 1024
ROW_TILE = 1024
NORM_ROWS = 256
IN_CHUNK = 256
FFN_CHUNK = 512
MIX_ROWS = 512
DOWN_COLS = 512
FINISH_ROWS = 256
OUT_ROWS = 512

assert N_META >= max(POOL_WINDOWS) and N_META >= CONV_K
assert POOL_WINDOWS == (2, 4, 8, 16)


def _dot(a, b):
    return lax.dot_general(a, b, (((1,), (0,)), ((), ())), preferred_element_type=F32)


def _rms(xf, g):
    ms = jnp.mean(xf * xf, axis=-1, keepdims=True)
    return xf * lax.rsqrt(ms + EPS) * g


def _params(vmem_mib, semantics):
    return pltpu.CompilerParams(dimension_semantics=semantics,
                                vmem_limit_bytes=vmem_mib * MIB)


def _fold_pool_kernel(wu_ref, pw_ref, ps_ref, o_ref):
    o_ref[...] = (_dot(wu_ref[...], pw_ref[0]) * ps_ref[...]).astype(BF16)


def _fold_pool(w_in, pool_w, pool_scale):
    d = w_in.shape[0]
    pg = POOL_GROUP
    return pl.pallas_call(
        _fold_pool_kernel,
        grid=(len(POOL_WINDOWS),),
        in_specs=[pl.BlockSpec((d, pg), lambda g: (0, g)),
                  pl.BlockSpec((1, pg, pg), lambda g: (g, 0, 0)),
                  pl.BlockSpec((1, pg), lambda g: (0, g))],
        out_specs=pl.BlockSpec((d, pg), lambda g: (0, g)),
        out_shape=jax.ShapeDtypeStruct((d, d), BF16),
        compiler_params=_params(32, ("arbitrary",)),
        name="fold_pool",
    )(w_in, pool_w, pool_scale)


def _inproj_kernel(meta_ref, g_ref, wp_ref, wga_ref, wgb_ref, wgc_ref, wv_ref, wgr_ref, bga_ref, bgr_ref,
                   cw_ref, cow_ref, wo_ref, x_hbm, gaya_ref, z_ref, gs_ref, cow_bf_ref, wo_bf_ref,
                   hn_s, stage, sem_x):
    bi = pl.program_id(0)
    chunk = pl.program_id(1)
    last_b = pl.num_programs(0) - 1
    n_pieces = (hn_s.shape[1] - N_META) // NORM_ROWS
    slot = bi % 2
    g = g_ref[...]
    cow_bf_ref[...] = cow_ref[...].astype(BF16)
    wo_bf_ref[...] = wo_ref[...].astype(BF16)

    def x_copy(batch, piece, buf):
        rows = pl.ds(piece * NORM_ROWS, NORM_ROWS)
        return pltpu.make_async_copy(x_hbm.at[batch, rows, :], stage.at[buf], sem_x.at[buf])

    def norm_piece(dst_slot, piece, buf):
        rows = pl.ds(pl.multiple_of(N_META + piece * NORM_ROWS, N_META), NORM_ROWS)
        hn_s[dst_slot, rows, :] = _rms(stage[buf], g).astype(BF16)

    @pl.when((bi == 0) & (chunk == 0))
    def _():
        meta_n = _rms(meta_ref[...], g).astype(BF16)
        hn_s[0, 0:N_META, :] = meta_n
        hn_s[1, 0:N_META, :] = meta_n
        x_copy(0, 0, 0).start()
        for piece in range(n_pieces):
            if piece + 1 < n_pieces:
                x_copy(0, piece + 1, (piece + 1) % 2).start()
            x_copy(0, piece, piece % 2).wait()
            norm_piece(0, piece, piece % 2)

    @pl.when(bi < last_b)
    def _():
        x_copy(bi + 1, chunk, 0).start()

    grp = chunk // (POOL_GROUP // IN_CHUNK)
    w0 = cw_ref[0:1, :]
    w1 = cw_ref[1:2, :]
    w2 = cw_ref[2:3, :]
    inv_window = jnp.where(grp == 0, 0.5, jnp.where(grp == 1, 0.25, jnp.where(grp == 2, 0.125, 0.0625)))
    n_tiles = (hn_s.shape[1] - N_META) // ROW_TILE

    def body(i, carry):
        r = pl.multiple_of(i * ROW_TILE, ROW_TILE)
        lhs = hn_s[slot, pl.ds(r, ROW_TILE + N_META), :]
        lhs_out = lhs[N_META:, :]

        y = _dot(lhs, wp_ref[...])
        cv = _dot(lhs, wgc_ref[...]) * _dot(lhs, wv_ref[...])
        s = y + pltpu.roll(y, 1, 0)
        s = jnp.where(grp >= 1, s + pltpu.roll(s, 2, 0), s)
        s = jnp.where(grp >= 2, s + pltpu.roll(s, 4, 0), s)
        s = jnp.where(grp >= 3, s + pltpu.roll(s, 8, 0), s)
        pooled = (s * inv_window - y)[N_META:, :]

        conv = w0 * pltpu.roll(cv, 2, 0) + w1 * pltpu.roll(cv, 1, 0) + w2 * cv
        z = _dot(lhs_out, wgb_ref[...]) * conv[N_META:, :]
        z_ref[0, pl.ds(r, ROW_TILE), :] = z.astype(BF16)
        ga = _dot(lhs_out, wga_ref[...]) + bga_ref[...]
        gaya_ref[0, pl.ds(r, ROW_TILE), :] = (jax.nn.sigmoid(ga) * pooled).astype(BF16)
        gr = _dot(lhs_out, wgr_ref[...]) + bgr_ref[...]
        gs_ref[0, pl.ds(r, ROW_TILE), :] = jax.nn.sigmoid(gr).astype(BF16)
        return carry

    lax.fori_loop(0, n_tiles, body, 0)

    @pl.when(bi < last_b)
    def _():
        x_copy(bi + 1, chunk, 0).wait()
        norm_piece(1 - slot, chunk, 0)


def _inproj(x, meta, g, w_pool, w_in, b_gate, conv_w, conv_out_w, w_o):
    b, seq, d = x.shape
    cc = IN_CHUNK
    n_chunks = d // cc
    assert seq == n_chunks * NORM_ROWS
    slab = d // (b * n_chunks)
    slab_spec = pl.BlockSpec((slab, d), lambda bi, c: (bi * n_chunks + c, 0))
    wspec = lambda first: pl.BlockSpec((d, cc), lambda bi, c: (0, first + c))
    bspec = lambda first: pl.BlockSpec((1, cc), lambda bi, c: (0, first + c))
    out_spec = pl.BlockSpec((1, seq, cc), lambda bi, c: (bi, 0, c))
    return pl.pallas_call(
        _inproj_kernel,
        grid=(b, n_chunks),
        in_specs=[pl.BlockSpec((N_META, d), lambda bi, c: (0, 0)),
                  pl.BlockSpec((1, d), lambda bi, c: (0, 0)),
                  wspec(0),
                  wspec(4 * n_chunks), wspec(1 * n_chunks), wspec(2 * n_chunks), wspec(3 * n_chunks),
                  wspec(5 * n_chunks),
                  bspec(0), bspec(n_chunks),
                  pl.BlockSpec((CONV_K, cc), lambda bi, c: (0, c)),
                  slab_spec, slab_spec,
                  pl.BlockSpec(memory_space=pl.ANY)],
        out_specs=[out_spec, out_spec, out_spec, slab_spec, slab_spec],
        out_shape=[jax.ShapeDtypeStruct((b, seq, d), BF16)] * 3 + [jax.ShapeDtypeStruct((d, d), BF16)] * 2,
        scratch_shapes=[pltpu.VMEM((2, seq + N_META, d), BF16),
                        pltpu.VMEM((2, NORM_ROWS, d), F32),
                        pltpu.SemaphoreType.DMA((2,))],
        compiler_params=_params(60, ("arbitrary", "arbitrary")),
        name="inproj_mixers",
    )(meta, g, w_pool, w_in, w_in, w_in, w_in, w_in, b_gate, b_gate, conv_w, conv_out_w, w_o, x)


def _mix_kernel(z_ref, gaya_ref, gs_ref, x_ref, cw_ref, wo_ref, g_ref, h1_ref, hn2_ref):
    yb = _dot(z_ref[...], cw_ref[...])
    mix = gaya_ref[...].astype(F32) + gs_ref[...].astype(F32) * yb
    h1 = x_ref[...] + _dot(mix.astype(BF16), wo_ref[...])
    h1_ref[...] = h1
    hn2_ref[...] = _rms(h1, g_ref[...]).astype(BF16)


def _mix(z, gaya, gs, x2d, conv_out_w, w_o, g):
    m, d = x2d.shape
    rows = pl.BlockSpec((MIX_ROWS, d), lambda i: (i, 0))
    whole = pl.BlockSpec((d, d), lambda i: (0, 0), pipeline_mode=pl.Buffered(1))
    return pl.pallas_call(
        _mix_kernel,
        grid=(m // MIX_ROWS,),
        in_specs=[rows, rows, rows, rows, whole, whole, pl.BlockSpec((1, d), lambda i: (0, 0))],
        out_specs=[rows, rows],
        out_shape=[jax.ShapeDtypeStruct((m, d), F32), jax.ShapeDtypeStruct((m, d), BF16)],
        compiler_params=_params(56, ("arbitrary",)),
        name="mix_out_proj",
    )(z, gaya, gs, x2d, conv_out_w, w_o, g)


def _ffn_kernel(wg_ref, wu_ref, wd_ref, g_ref, hn_hbm, h1_hbm, out_hbm,
                hn_s, acc, stage, sem_hn, sem_h1, sem_out):
    bi = pl.program_id(0)
    j = pl.program_id(1)
    last_b = pl.num_programs(0) - 1
    last_j = pl.num_programs(1) - 1
    seq, d = acc.shape
    n_tiles = seq // ROW_TILE
    n_fin = seq // FINISH_ROWS
    n_out = seq // OUT_ROWS
    fin_per_out = OUT_ROWS // FINISH_ROWS

    def hn_copy(batch, t):
        rows = pl.ds(t * ROW_TILE, ROW_TILE)
        return pltpu.make_async_copy(hn_hbm.at[batch, rows, :], hn_s.at[rows, :], sem_hn.at[t])

    def h1_copy(s):
        rows = pl.ds(s * FINISH_ROWS, FINISH_ROWS)
        return pltpu.make_async_copy(h1_hbm.at[bi, rows, :], stage.at[s % 2], sem_h1.at[s % 2])

    def out_copy(batch, p):
        rows = pl.ds(p * OUT_ROWS, OUT_ROWS)
        return pltpu.make_async_copy(acc.at[rows, :], out_hbm.at[batch, rows, :], sem_out.at[p])

    @pl.when((bi == 0) & (j == 0))
    def _():
        for t in range(n_tiles):
            hn_copy(0, t).start()
        acc[...] = jnp.zeros_like(acc)

    @pl.when(j == 0)
    def _():
        for t in range(n_tiles):
            hn_copy(bi, t).wait()

        @pl.when(bi > 0)
        def _():
            for p in range(n_out):
                out_copy(bi - 1, p).wait()

    @pl.when(j < n_fin)
    def _():
        h1_copy(j).start()

    def tile_body(t, carry):
        rows = pl.ds(pl.multiple_of(t * ROW_TILE, ROW_TILE), ROW_TILE)
        lhs = hn_s[rows, :]
        gate = _dot(lhs, wg_ref[...])
        up = _dot(lhs, wu_ref[...])
        act = (jax.nn.silu(gate) * up).astype(BF16)
        for n in range(d // DOWN_COLS):
            cols = slice(n * DOWN_COLS, (n + 1) * DOWN_COLS)
            prev = jnp.where(j == 0, 0.0, acc[rows, cols])
            acc[rows, cols] = prev + _dot(act, wd_ref[:, cols])
        return carry

    lax.fori_loop(0, n_tiles, tile_body, 0)

    @pl.when((j == last_j) & (bi < last_b))
    def _():
        for t in range(n_tiles):
            hn_copy(bi + 1, t).start()

    @pl.when((j >= 1) & (j <= n_fin))
    def _():
        piece = j - 1
        h1_copy(piece).wait()
        rows = pl.ds(pl.multiple_of(piece * FINISH_ROWS, FINISH_ROWS), FINISH_ROWS)
        acc[rows, :] += stage[piece % 2]

    @pl.when(j == last_j)
    def _():
        for s in range(n_fin):
            rows = pl.ds(s * FINISH_ROWS, FINISH_ROWS)
            acc[rows, :] = _rms(acc[rows, :], g_ref[...])
            if (s + 1) % fin_per_out == 0:
                out_copy(bi, s // fin_per_out).start()

        @pl.when(bi == last_b)
        def _():
            for p in range(n_out):
                out_copy(bi, p).wait()


def _ffn(hn2, h1, w_gate_up, w_down, g):
    b, seq, d = hn2.shape
    fc = FFN_CHUNK
    n_chunks = FFN_HIDDEN // fc
    n_tiles = seq // ROW_TILE
    assert n_chunks > seq // FINISH_ROWS
    return pl.pallas_call(
        _ffn_kernel,
        grid=(b, n_chunks),
        in_specs=[pl.BlockSpec((d, fc), lambda bi, j: (0, j)),
                  pl.BlockSpec((d, fc), lambda bi, j: (0, n_chunks + j)),
                  pl.BlockSpec((fc, d), lambda bi, j: (j, 0)),
                  pl.BlockSpec((1, d), lambda bi, j: (0, 0)),
                  pl.BlockSpec(memory_space=pl.ANY),
                  pl.BlockSpec(memory_space=pl.ANY)],
        out_specs=pl.BlockSpec(memory_space=pl.ANY),
        out_shape=jax.ShapeDtypeStruct((b, seq, d), F32),
        scratch_shapes=[pltpu.VMEM((seq, d), BF16),
                        pltpu.VMEM((seq, d), F32),
                        pltpu.VMEM((2, FINISH_ROWS, d), F32),
                        pltpu.SemaphoreType.DMA((n_tiles,)),
                        pltpu.SemaphoreType.DMA((2,)),
                        pltpu.SemaphoreType.DMA((seq // OUT_ROWS,))],
        compiler_params=_params(60, ("arbitrary", "arbitrary")),
        name="ffn",
    )(w_gate_up, w_gate_up, w_down, g, hn2, h1)


def kernel(x, meta_tokens, norm_mix_g, w_in, b_gate, pool_w, pool_scale, conv_w, conv_out_w, w_o,
           norm_ffn_g, w_gate_up, w_down, norm_final_g):
    b, seq, d = x.shape
    row = lambda v: v.reshape(1, -1)
    w_pool = _fold_pool(w_in, pool_w, row(pool_scale))
    gaya, z, gs, conv_out_bf, w_o_bf = _inproj(x, meta_tokens, row(norm_mix_g), w_pool, w_in, row(b_gate),
                                               conv_w, conv_out_w, w_o)
    m = b * seq
    h1, hn2 = _mix(z.reshape(m, d), gaya.reshape(m, d), gs.reshape(m, d), x.reshape(m, d),
                   conv_out_bf, w_o_bf, row(norm_ffn_g))
    return _ffn(hn2.reshape(b, seq, d), h1.reshape(b, seq, d), w_gate_up, w_down, row(norm_final_g))
```

```python
import math

import jax
import jax.numpy as jnp
from jax import lax
from jax.experimental import pallas as pl
from jax.experimental.pallas import tpu as pltpu

D_MODEL = 2048
N_META = 16
POOL_WINDOWS = (2, 4, 8, 16)
POOL_GROUP = D_MODEL // len(POOL_WINDOWS)
CONV_K = 3
FFN_HIDDEN = ((8 * D_MODEL // 3 + 255) // 256) * 256
EPS = 1e-6

F32 = jnp.float32
BF16 = jnp.bfloat16

MIB = 1024 * 1024
V7X_VMEM_BYTES = 64 * MIB
COMPILER_TEMP_BYTES = 8 * MIB
ROW_TILE = 1024
NORM_ROWS = 256
IN_CHUNK = 256
FFN_CHUNK = 512
MIX_ROWS = 512
DOWN_COLS = 512
FINISH_ROWS = 256

assert N_META >= max(POOL_WINDOWS) and N_META >= CONV_K
assert POOL_WINDOWS == (2, 4, 8, 16)


def _dot(a, b):
    return lax.dot_general(a, b, (((1,), (0,)), ((), ())), preferred_element_type=F32)


def _rms(xf, g):
    ms = jnp.mean(xf * xf, axis=-1, keepdims=True)
    return xf * lax.rsqrt(ms + EPS) * g


def _nbytes(shape, dtype):
    return math.prod(shape) * jnp.dtype(dtype).itemsize


def _params(semantics, windows, scratch=()):
    declared = sum(_nbytes(s, t) * n for s, t, n in windows) + sum(_nbytes(s, t) for s, t in scratch)
    limit = declared + COMPILER_TEMP_BYTES
    assert limit <= V7X_VMEM_BYTES, (declared, limit)
    return pltpu.CompilerParams(dimension_semantics=semantics, vmem_limit_bytes=limit)


def _fold_pool_kernel(wu_ref, pw_ref, ps_ref, o_ref):
    o_ref[...] = (_dot(wu_ref[...], pw_ref[0]) * ps_ref[...]).astype(BF16)


def _fold_pool(w_in, pool_w, pool_scale):
    d = w_in.shape[0]
    pg = POOL_GROUP
    return pl.pallas_call(
        _fold_pool_kernel,
        grid=(len(POOL_WINDOWS),),
        in_specs=[pl.BlockSpec((d, pg), lambda g: (0, g)),
                  pl.BlockSpec((1, pg, pg), lambda g: (g, 0, 0)),
                  pl.BlockSpec((1, pg), lambda g: (0, g))],
        out_specs=pl.BlockSpec((d, pg), lambda g: (0, g)),
        out_shape=jax.ShapeDtypeStruct((d, d), BF16),
        compiler_params=_params(("arbitrary",),
                                [((d, pg), F32, 2), ((pg, pg), F32, 2), ((d, pg), BF16, 2)]),
        name="fold_pool",
    )(w_in, pool_w, pool_scale)


def _inproj_kernel(meta_ref, g_ref, wp_ref, wga_ref, wgb_ref, wgc_ref, wv_ref, wgr_ref, bga_ref, bgr_ref,
                   cw_ref, cow_ref, wo_ref, x_hbm, gaya_ref, z_ref, gs_ref, cow_bf_ref, wo_bf_ref,
                   hn_s, stage, sem_x):
    bi = pl.program_id(0)
    chunk = pl.program_id(1)
    last_b = pl.num_programs(0) - 1
    n_pieces = (hn_s.shape[1] - N_META) // NORM_ROWS
    slot = bi % 2
    g = g_ref[...]
    cow_bf_ref[...] = cow_ref[...].astype(BF16)
    wo_bf_ref[...] = wo_ref[...].astype(BF16)

    def x_copy(batch, piece, buf):
        rows = pl.ds(piece * NORM_ROWS, NORM_ROWS)
        return pltpu.make_async_copy(x_hbm.at[batch, rows, :], stage.at[buf], sem_x.at[buf])

    def norm_piece(dst_slot, piece, buf):
        rows = pl.ds(pl.multiple_of(N_META + piece * NORM_ROWS, N_META), NORM_ROWS)
        hn_s[dst_slot, rows, :] = _rms(stage[buf], g).astype(BF16)

    @pl.when((bi == 0) & (chunk == 0))
    def _():
        meta_n = _rms(meta_ref[...], g).astype(BF16)
        hn_s[0, 0:N_META, :] = meta_n
        hn_s[1, 0:N_META, :] = meta_n
        x_copy(0, 0, 0).start()
        for piece in range(n_pieces):
            if piece + 1 < n_pieces:
                x_copy(0, piece + 1, (piece + 1) % 2).start()
            x_copy(0, piece, piece % 2).wait()
            norm_piece(0, piece, piece % 2)

    @pl.when(bi < last_b)
    def _():
        x_copy(bi + 1, chunk, 0).start()

    grp = chunk // (POOL_GROUP // IN_CHUNK)
    w0 = cw_ref[0:1, :]
    w1 = cw_ref[1:2, :]
    w2 = cw_ref[2:3, :]
    inv_window = jnp.where(grp == 0, 0.5, jnp.where(grp == 1, 0.25, jnp.where(grp == 2, 0.125, 0.0625)))
    n_tiles = (hn_s.shape[1] - N_META) // ROW_TILE

    def body(i, carry):
        r = pl.multiple_of(i * ROW_TILE, ROW_TILE)
        lhs = hn_s[slot, pl.ds(r, ROW_TILE + N_META), :]
        lhs_out = lhs[N_META:, :]

        y = _dot(lhs, wp_ref[...])
        cv = _dot(lhs, wgc_ref[...]) * _dot(lhs, wv_ref[...])
        s = y + pltpu.roll(y, 1, 0)
        s = jnp.where(grp >= 1, s + pltpu.roll(s, 2, 0), s)
        s = jnp.where(grp >= 2, s + pltpu.roll(s, 4, 0), s)
        s = jnp.where(grp >= 3, s + pltpu.roll(s, 8, 0), s)
        pooled = (s * inv_window - y)[N_META:, :]

        conv = w0 * pltpu.roll(cv, 2, 0) + w1 * pltpu.roll(cv, 1, 0) + w2 * cv
        z = _dot(lhs_out, wgb_ref[...]) * conv[N_META:, :]
        z_ref[0, pl.ds(r, ROW_TILE), :] = z.astype(BF16)
        ga = _dot(lhs_out, wga_ref[...]) + bga_ref[...]
        gaya_ref[0, pl.ds(r, ROW_TILE), :] = (jax.nn.sigmoid(ga) * pooled).astype(BF16)
        gr = _dot(lhs_out, wgr_ref[...]) + bgr_ref[...]
        gs_ref[0, pl.ds(r, ROW_TILE), :] = jax.nn.sigmoid(gr).astype(BF16)
        return carry

    lax.fori_loop(0, n_tiles, body, 0)

    @pl.when(bi < last_b)
    def _():
        x_copy(bi + 1, chunk, 0).wait()
        norm_piece(1 - slot, chunk, 0)


def _inproj(x, meta, g, w_pool, w_in, b_gate, conv_w, conv_out_w, w_o):
    b, seq, d = x.shape
    cc = IN_CHUNK
    n_chunks = d // cc
    assert seq == n_chunks * NORM_ROWS
    slab = d // (b * n_chunks)
    slab_spec = pl.BlockSpec((slab, d), lambda bi, c: (bi * n_chunks + c, 0))
    wspec = lambda first: pl.BlockSpec((d, cc), lambda bi, c: (0, first + c))
    bspec = lambda first: pl.BlockSpec((1, cc), lambda bi, c: (0, first + c))
    out_spec = pl.BlockSpec((1, seq, cc), lambda bi, c: (bi, 0, c))
    hn_scratch = ((2, seq + N_META, d), BF16)
    x_stage = ((2, NORM_ROWS, d), F32)
    return pl.pallas_call(
        _inproj_kernel,
        grid=(b, n_chunks),
        in_specs=[pl.BlockSpec((N_META, d), lambda bi, c: (0, 0)),
                  pl.BlockSpec((1, d), lambda bi, c: (0, 0)),
                  wspec(0),
                  wspec(4 * n_chunks), wspec(1 * n_chunks), wspec(2 * n_chunks), wspec(3 * n_chunks),
                  wspec(5 * n_chunks),
                  bspec(0), bspec(n_chunks),
                  pl.BlockSpec((CONV_K, cc), lambda bi, c: (0, c)),
                  slab_spec, slab_spec,
                  pl.BlockSpec(memory_space=pl.ANY)],
        out_specs=[out_spec, out_spec, out_spec, slab_spec, slab_spec],
        out_shape=[jax.ShapeDtypeStruct((b, seq, d), BF16)] * 3 + [jax.ShapeDtypeStruct((d, d), BF16)] * 2,
        scratch_shapes=[pltpu.VMEM(*hn_scratch), pltpu.VMEM(*x_stage), pltpu.SemaphoreType.DMA((2,))],
        compiler_params=_params(
            ("arbitrary", "arbitrary"),
            [((d, cc), F32, 2)] * 5 + [((d, cc), BF16, 2), ((N_META, d), F32, 2)]
            + [((slab, d), F32, 2)] * 2 + [((slab, d), BF16, 2)] * 2 + [((seq, cc), BF16, 2)] * 3,
            [hn_scratch, x_stage]),
        name="inproj_mixers",
    )(meta, g, w_pool, w_in, w_in, w_in, w_in, w_in, b_gate, b_gate, conv_w, conv_out_w, w_o, x)


def _mix_kernel(z_ref, gaya_ref, gs_ref, x_ref, cw_ref, wo_ref, g_ref, h1_ref, hn2_ref):
    yb = _dot(z_ref[...], cw_ref[...])
    mix = gaya_ref[...].astype(F32) + gs_ref[...].astype(F32) * yb
    h1 = x_ref[...] + _dot(mix.astype(BF16), wo_ref[...])
    h1_ref[...] = h1
    hn2_ref[...] = _rms(h1, g_ref[...]).astype(BF16)


def _mix(z, gaya, gs, x2d, conv_out_w, w_o, g):
    m, d = x2d.shape
    rows = pl.BlockSpec((MIX_ROWS, d), lambda i: (i, 0))
    whole = pl.BlockSpec((d, d), lambda i: (0, 0), pipeline_mode=pl.Buffered(1))
    return pl.pallas_call(
        _mix_kernel,
        grid=(m // MIX_ROWS,),
        in_specs=[rows, rows, rows, rows, whole, whole, pl.BlockSpec((1, d), lambda i: (0, 0))],
        out_specs=[rows, rows],
        out_shape=[jax.ShapeDtypeStruct((m, d), F32), jax.ShapeDtypeStruct((m, d), BF16)],
        compiler_params=_params(
            ("arbitrary",),
            [((MIX_ROWS, d), BF16, 2)] * 3 + [((MIX_ROWS, d), F32, 2)] + [((d, d), BF16, 1)] * 2
            + [((MIX_ROWS, d), F32, 2), ((MIX_ROWS, d), BF16, 2)]),
        name="mix_out_proj",
    )(z, gaya, gs, x2d, conv_out_w, w_o, g)


def _ffn_kernel(wg_ref, wu_ref, wd_ref, g_ref, hn_hbm, h1_hbm, out_hbm,
                hn_s, acc, stage, sem_hn, sem_h1, sem_out):
    bi = pl.program_id(0)
    j = pl.program_id(1)
    last_b = pl.num_programs(0) - 1
    last_j = pl.num_programs(1) - 1
    seq, d = acc.shape
    n_tiles = seq // ROW_TILE
    n_fin = seq // FINISH_ROWS
    fin_per_tile = ROW_TILE // FINISH_ROWS

    def hn_copy(batch, t):
        rows = pl.ds(t * ROW_TILE, ROW_TILE)
        return pltpu.make_async_copy(hn_hbm.at[batch, rows, :], hn_s.at[rows, :], sem_hn.at[t])

    def h1_copy(s):
        rows = pl.ds(s * FINISH_ROWS, FINISH_ROWS)
        return pltpu.make_async_copy(h1_hbm.at[bi, rows, :], stage.at[s % 2], sem_h1.at[s % 2])

    def out_copy(batch, t):
        rows = pl.ds(t * ROW_TILE, ROW_TILE)
        return pltpu.make_async_copy(acc.at[rows, :], out_hbm.at[batch, rows, :], sem_out.at[t])

    @pl.when((bi == 0) & (j == 0))
    def _():
        for t in range(n_tiles):
            hn_copy(0, t).start()
        acc[...] = jnp.zeros_like(acc)

    @pl.when(j < n_fin)
    def _():
        h1_copy(j).start()

    for t in range(n_tiles):
        rows = pl.ds(t * ROW_TILE, ROW_TILE)

        @pl.when(j == 0)
        def _():
            hn_copy(bi, t).wait()

            @pl.when(bi > 0)
            def _():
                out_copy(bi - 1, t).wait()

        lhs = hn_s[rows, :]
        gate = _dot(lhs, wg_ref[...])
        up = _dot(lhs, wu_ref[...])
        act = (jax.nn.silu(gate) * up).astype(BF16)
        for n in range(d // DOWN_COLS):
            cols = slice(n * DOWN_COLS, (n + 1) * DOWN_COLS)
            prev = jnp.where(j == 0, 0.0, acc[rows, cols])
            acc[rows, cols] = prev + _dot(act, wd_ref[:, cols])

        @pl.when((j == last_j) & (bi < last_b))
        def _():
            hn_copy(bi + 1, t).start()

    @pl.when((j >= 1) & (j <= n_fin))
    def _():
        piece = j - 1
        h1_copy(piece).wait()
        rows = pl.ds(pl.multiple_of(piece * FINISH_ROWS, FINISH_ROWS), FINISH_ROWS)
        acc[rows, :] += stage[piece % 2]

    @pl.when(j == last_j)
    def _():
        for s in range(n_fin):
            rows = pl.ds(s * FINISH_ROWS, FINISH_ROWS)
            acc[rows, :] = _rms(acc[rows, :], g_ref[...])
            if (s + 1) % fin_per_tile == 0:
                out_copy(bi, s // fin_per_tile).start()

        @pl.when(bi == last_b)
        def _():
            for t in range(n_tiles):
                out_copy(bi, t).wait()


def _ffn(hn2, h1, w_gate_up, w_down, g):
    b, seq, d = hn2.shape
    fc = FFN_CHUNK
    n_chunks = FFN_HIDDEN // fc
    n_tiles = seq // ROW_TILE
    assert n_chunks > seq // FINISH_ROWS
    vmem_scratch = [((seq, d), BF16), ((seq, d), F32), ((2, FINISH_ROWS, d), F32)]
    return pl.pallas_call(
        _ffn_kernel,
        grid=(b, n_chunks),
        in_specs=[pl.BlockSpec((d, fc), lambda bi, j: (0, j)),
                  pl.BlockSpec((d, fc), lambda bi, j: (0, n_chunks + j)),
                  pl.BlockSpec((fc, d), lambda bi, j: (j, 0)),
                  pl.BlockSpec((1, d), lambda bi, j: (0, 0)),
                  pl.BlockSpec(memory_space=pl.ANY),
                  pl.BlockSpec(memory_space=pl.ANY)],
        out_specs=pl.BlockSpec(memory_space=pl.ANY),
        out_shape=jax.ShapeDtypeStruct((b, seq, d), F32),
        scratch_shapes=[pltpu.VMEM(*s) for s in vmem_scratch] + [
            pltpu.SemaphoreType.DMA((n_tiles,)),
            pltpu.SemaphoreType.DMA((2,)),
            pltpu.SemaphoreType.DMA((n_tiles,))],
        compiler_params=_params(("arbitrary", "arbitrary"),
                                [((d, fc), F32, 2)] * 2 + [((fc, d), F32, 2)], vmem_scratch),
        name="ffn",
    )(w_gate_up, w_gate_up, w_down, g, hn2, h1)


def kernel(x, meta_tokens, norm_mix_g, w_in, b_gate, pool_w, pool_scale, conv_w, conv_out_w, w_o,
           norm_ffn_g, w_gate_up, w_down, norm_final_g):
    b, seq, d = x.shape
    row = lambda v: v.reshape(1, -1)
    w_pool = _fold_pool(w_in, pool_w, row(pool_scale))
    gaya, z, gs, conv_out_bf, w_o_bf = _inproj(x, meta_tokens, row(norm_mix_g), w_pool, w_in, row(b_gate),
                                               conv_w, conv_out_w, w_o)
    m = b * seq
    h1, hn2 = _mix(z.reshape(m, d), gaya.reshape(m, d), gs.reshape(m, d), x.reshape(m, d),
                   conv_out_bf, w_o_bf, row(norm_ffn_g))
    return _ffn(hn2.reshape(b, seq, d), h1.reshape(b, seq, d), w_gate_up, w_down, row(norm_final_g))
```

```python
import math

import jax
import jax.numpy as jnp
from jax import lax
from jax.experimental import pallas as pl
from jax.experimental.pallas import tpu as pltpu

D_MODEL = 2048
N_META = 16
POOL_WINDOWS = (2, 4, 8, 16)
POOL_GROUP = D_MODEL // len(POOL_WINDOWS)
CONV_K = 3
FFN_HIDDEN = ((8 * D_MODEL // 3 + 255) // 256) * 256
EPS = 1e-6

F32 = jnp.float32
BF16 = jnp.bfloat16

MIB = 1024 * 1024
V7X_VMEM_BYTES = 64 * MIB
COMPILER_TEMP_BYTES = 8 * MIB
ROW_TILE = 1024
NORM_ROWS = 256
IN_CHUNK = 256
FFN_CHUNK = 512
MIX_ROWS = 512
DOWN_COLS = 512
FINISH_ROWS = 256

assert N_META >= max(POOL_WINDOWS) and N_META >= CONV_K
assert POOL_WINDOWS == (2, 4, 8, 16)


def _dot(a, b):
    return lax.dot_general(a, b, (((1,), (0,)), ((), ())), preferred_element_type=F32)


def _rms(xf, g):
    ms = jnp.mean(xf * xf, axis=-1, keepdims=True)
    return xf * lax.rsqrt(ms + EPS) * g


def _sigmoid(x):
    return 0.5 * (jnp.tanh(0.5 * x) + 1.0)


def _nbytes(shape, dtype):
    return math.prod(shape) * jnp.dtype(dtype).itemsize


def _params(semantics, windows, scratch=()):
    declared = sum(_nbytes(s, t) * n for s, t, n in windows) + sum(_nbytes(s, t) for s, t in scratch)
    limit = declared + COMPILER_TEMP_BYTES
    assert limit <= V7X_VMEM_BYTES, (declared, limit)
    return pltpu.CompilerParams(dimension_semantics=semantics, vmem_limit_bytes=limit)


def _fold_pool_kernel(wu_ref, pw_ref, ps_ref, o_ref):
    o_ref[...] = (_dot(wu_ref[...], pw_ref[0]) * ps_ref[...]).astype(BF16)


def _fold_pool(w_in, pool_w, pool_scale):
    d = w_in.shape[0]
    pg = POOL_GROUP
    return pl.pallas_call(
        _fold_pool_kernel,
        grid=(len(POOL_WINDOWS),),
        in_specs=[pl.BlockSpec((d, pg), lambda g: (0, g)),
                  pl.BlockSpec((1, pg, pg), lambda g: (g, 0, 0)),
                  pl.BlockSpec((1, pg), lambda g: (0, g))],
        out_specs=pl.BlockSpec((d, pg), lambda g: (0, g)),
        out_shape=jax.ShapeDtypeStruct((d, d), BF16),
        compiler_params=_params(("arbitrary",),
                                [((d, pg), F32, 2), ((pg, pg), F32, 2), ((d, pg), BF16, 2)]),
        name="fold_pool",
    )(w_in, pool_w, pool_scale)


def _inproj_kernel(meta_ref, g_ref, wp_ref, wga_ref, wgb_ref, wgc_ref, wv_ref, wgr_ref, bga_ref, bgr_ref,
                   cw_ref, cow_ref, wo_ref, x_hbm, gaya_ref, z_ref, gs_ref, cow_bf_ref, wo_bf_ref,
                   hn_s, stage, sem_x):
    bi = pl.program_id(0)
    chunk = pl.program_id(1)
    last_b = pl.num_programs(0) - 1
    n_pieces = (hn_s.shape[1] - N_META) // NORM_ROWS
    slot = bi % 2
    g = g_ref[...]
    cow_bf_ref[...] = cow_ref[...].astype(BF16)
    wo_bf_ref[...] = wo_ref[...].astype(BF16)

    def x_copy(batch, piece, buf):
        rows = pl.ds(piece * NORM_ROWS, NORM_ROWS)
        return pltpu.make_async_copy(x_hbm.at[batch, rows, :], stage.at[buf], sem_x.at[buf])

    def norm_piece(dst_slot, piece, buf):
        rows = pl.ds(pl.multiple_of(N_META + piece * NORM_ROWS, N_META), NORM_ROWS)
        hn_s[dst_slot, rows, :] = _rms(stage[buf], g).astype(BF16)

    @pl.when((bi == 0) & (chunk == 0))
    def _():
        meta_n = _rms(meta_ref[...], g).astype(BF16)
        hn_s[0, 0:N_META, :] = meta_n
        hn_s[1, 0:N_META, :] = meta_n
        x_copy(0, 0, 0).start()
        for piece in range(n_pieces):
            if piece + 1 < n_pieces:
                x_copy(0, piece + 1, (piece + 1) % 2).start()
            x_copy(0, piece, piece % 2).wait()
            norm_piece(0, piece, piece % 2)

    @pl.when(bi < last_b)
    def _():
        x_copy(bi + 1, chunk, 0).start()

    grp = chunk // (POOL_GROUP // IN_CHUNK)
    w0 = cw_ref[0:1, :]
    w1 = cw_ref[1:2, :]
    w2 = cw_ref[2:3, :]
    inv_window = jnp.where(grp == 0, 0.5, jnp.where(grp == 1, 0.25, jnp.where(grp == 2, 0.125, 0.0625)))
    n_tiles = (hn_s.shape[1] - N_META) // ROW_TILE

    def body(i, carry):
        r = pl.multiple_of(i * ROW_TILE, ROW_TILE)
        lhs = hn_s[slot, pl.ds(r, ROW_TILE + N_META), :]
        lhs_out = lhs[N_META:, :]

        y = _dot(lhs, wp_ref[...])
        cv = _dot(lhs, wgc_ref[...]) * _dot(lhs, wv_ref[...])
        s = y + pltpu.roll(y, 1, 0)
        s = jnp.where(grp >= 1, s + pltpu.roll(s, 2, 0), s)
        s = jnp.where(grp >= 2, s + pltpu.roll(s, 4, 0), s)
        s = jnp.where(grp >= 3, s + pltpu.roll(s, 8, 0), s)
        pooled = (s * inv_window - y)[N_META:, :]

        conv = w0 * pltpu.roll(cv, 2, 0) + w1 * pltpu.roll(cv, 1, 0) + w2 * cv
        ga = _dot(lhs_out, wga_ref[...]) + bga_ref[...]
        gaya_ref[0, pl.ds(r, ROW_TILE), :] = (_sigmoid(ga) * pooled).astype(BF16)
        gr = _dot(lhs_out, wgr_ref[...]) + bgr_ref[...]
        gs_ref[0, pl.ds(r, ROW_TILE), :] = _sigmoid(gr).astype(BF16)
        z = _dot(lhs_out, wgb_ref[...]) * conv[N_META:, :]
        z_ref[0, pl.ds(r, ROW_TILE), :] = z.astype(BF16)
        return carry

    lax.fori_loop(0, n_tiles, body, 0)

    @pl.when(bi < last_b)
    def _():
        x_copy(bi + 1, chunk, 0).wait()
        norm_piece(1 - slot, chunk, 0)


def _inproj(x, meta, g, w_pool, w_in, b_gate, conv_w, conv_out_w, w_o):
    b, seq, d = x.shape
    cc = IN_CHUNK
    n_chunks = d // cc
    assert seq == n_chunks * NORM_ROWS
    slab = d // (b * n_chunks)
    slab_spec = pl.BlockSpec((slab, d), lambda bi, c: (bi * n_chunks + c, 0))
    wspec = lambda first: pl.BlockSpec((d, cc), lambda bi, c: (0, first + c))
    bspec = lambda first: pl.BlockSpec((1, cc), lambda bi, c: (0, first + c))
    out_spec = pl.BlockSpec((1, seq, cc), lambda bi, c: (bi, 0, c))
    hn_scratch = ((2, seq + N_META, d), BF16)
    x_stage = ((2, NORM_ROWS, d), F32)
    return pl.pallas_call(
        _inproj_kernel,
        grid=(b, n_chunks),
        in_specs=[pl.BlockSpec((N_META, d), lambda bi, c: (0, 0)),
                  pl.BlockSpec((1, d), lambda bi, c: (0, 0)),
                  wspec(0),
                  wspec(4 * n_chunks), wspec(1 * n_chunks), wspec(2 * n_chunks), wspec(3 * n_chunks),
                  wspec(5 * n_chunks),
                  bspec(0), bspec(n_chunks),
                  pl.BlockSpec((CONV_K, cc), lambda bi, c: (0, c)),
                  slab_spec, slab_spec,
                  pl.BlockSpec(memory_space=pl.ANY)],
        out_specs=[out_spec, out_spec, out_spec, slab_spec, slab_spec],
        out_shape=[jax.ShapeDtypeStruct((b, seq, d), BF16)] * 3 + [jax.ShapeDtypeStruct((d, d), BF16)] * 2,
        scratch_shapes=[pltpu.VMEM(*hn_scratch), pltpu.VMEM(*x_stage), pltpu.SemaphoreType.DMA((2,))],
        compiler_params=_params(
            ("arbitrary", "arbitrary"),
            [((d, cc), F32, 2)] * 5 + [((d, cc), BF16, 2), ((N_META, d), F32, 2)]
            + [((slab, d), F32, 2)] * 2 + [((slab, d), BF16, 2)] * 2 + [((seq, cc), BF16, 2)] * 3,
            [hn_scratch, x_stage]),
        name="inproj_mixers",
    )(meta, g, w_pool, w_in, w_in, w_in, w_in, w_in, b_gate, b_gate, conv_w, conv_out_w, w_o, x)


def _mix_kernel(z_ref, gaya_ref, gs_ref, x_ref, cw_ref, wo_ref, g_ref, h1_ref, hn2_ref):
    yb = _dot(z_ref[...], cw_ref[...])
    mix = gaya_ref[...].astype(F32) + gs_ref[...].astype(F32) * yb
    h1 = x_ref[...] + _dot(mix.astype(BF16), wo_ref[...])
    h1_ref[...] = h1
    hn2_ref[...] = _rms(h1, g_ref[...]).astype(BF16)


def _mix(z, gaya, gs, x2d, conv_out_w, w_o, g):
    m, d = x2d.shape
    rows = pl.BlockSpec((MIX_ROWS, d), lambda i: (i, 0))
    whole = pl.BlockSpec((d, d), lambda i: (0, 0), pipeline_mode=pl.Buffered(1))
    return pl.pallas_call(
        _mix_kernel,
        grid=(m // MIX_ROWS,),
        in_specs=[rows, rows, rows, rows, whole, whole, pl.BlockSpec((1, d), lambda i: (0, 0))],
        out_specs=[rows, rows],
        out_shape=[jax.ShapeDtypeStruct((m, d), F32), jax.ShapeDtypeStruct((m, d), BF16)],
        compiler_params=_params(
            ("arbitrary",),
            [((MIX_ROWS, d), BF16, 2)] * 3 + [((MIX_ROWS, d), F32, 2)] + [((d, d), BF16, 1)] * 2
            + [((MIX_ROWS, d), F32, 2), ((MIX_ROWS, d), BF16, 2)]),
        name="mix_out_proj",
    )(z, gaya, gs, x2d, conv_out_w, w_o, g)


def _ffn_kernel(wg_ref, wu_ref, wd_ref, g_ref, hn_hbm, h1_hbm, out_hbm,
                hn_s, acc, stage, sem_hn, sem_h1, sem_out):
    bi = pl.program_id(0)
    j = pl.program_id(1)
    last_b = pl.num_programs(0) - 1
    last_j = pl.num_programs(1) - 1
    seq, d = acc.shape
    n_tiles = seq // ROW_TILE
    n_fin = seq // FINISH_ROWS
    fin_per_tile = ROW_TILE // FINISH_ROWS

    def hn_copy(batch, t):
        rows = pl.ds(t * ROW_TILE, ROW_TILE)
        return pltpu.make_async_copy(hn_hbm.at[batch, rows, :], hn_s.at[rows, :], sem_hn.at[t])

    def h1_copy(s):
        rows = pl.ds(s * FINISH_ROWS, FINISH_ROWS)
        return pltpu.make_async_copy(h1_hbm.at[bi, rows, :], stage.at[s % 2], sem_h1.at[s % 2])

    def out_copy(batch, t):
        rows = pl.ds(t * ROW_TILE, ROW_TILE)
        return pltpu.make_async_copy(acc.at[rows, :], out_hbm.at[batch, rows, :], sem_out.at[t])

    @pl.when((bi == 0) & (j == 0))
    def _():
        for t in range(n_tiles):
            hn_copy(0, t).start()
        acc[...] = jnp.zeros_like(acc)

    @pl.when(j < n_fin)
    def _():
        h1_copy(j).start()

    for t in range(n_tiles):
        rows = pl.ds(t * ROW_TILE, ROW_TILE)

        @pl.when(j == 0)
        def _():
            hn_copy(bi, t).wait()

            @pl.when(bi > 0)
            def _():
                out_copy(bi - 1, t).wait()

        lhs = hn_s[rows, :]
        gate = _dot(lhs, wg_ref[...])
        up = _dot(lhs, wu_ref[...])
        half_gate = 0.5 * gate
        act = (half_gate * (1.0 + jnp.tanh(half_gate)) * up).astype(BF16)
        for n in range(d // DOWN_COLS):
            cols = slice(n * DOWN_COLS, (n + 1) * DOWN_COLS)
            prev = jnp.where(j == 0, 0.0, acc[rows, cols])
            acc[rows, cols] = prev + _dot(act, wd_ref[:, cols])

        @pl.when((j == last_j) & (bi < last_b))
        def _():
            hn_copy(bi + 1, t).start()

    @pl.when((j >= 1) & (j <= n_fin))
    def _():
        piece = j - 1
        h1_copy(piece).wait()
        rows = pl.ds(pl.multiple_of(piece * FINISH_ROWS, FINISH_ROWS), FINISH_ROWS)
        acc[rows, :] += stage[piece % 2]

    @pl.when(j == last_j)
    def _():
        for s in range(n_fin):
            rows = pl.ds(s * FINISH_ROWS, FINISH_ROWS)
            acc[rows, :] = _rms(acc[rows, :], g_ref[...])
            if (s + 1) % fin_per_tile == 0:
                out_copy(bi, s // fin_per_tile).start()

        @pl.when(bi == last_b)
        def _():
            for t in range(n_tiles):
                out_copy(bi, t).wait()


def _ffn(hn2, h1, w_gate_up, w_down, g):
    b, seq, d = hn2.shape
    fc = FFN_CHUNK
    n_chunks = FFN_HIDDEN // fc
    n_tiles = seq // ROW_TILE
    assert n_chunks > seq // FINISH_ROWS
    vmem_scratch = [((seq, d), BF16), ((seq, d), F32), ((2, FINISH_ROWS, d), F32)]
    return pl.pallas_call(
        _ffn_kernel,
        grid=(b, n_chunks),
        in_specs=[pl.BlockSpec((d, fc), lambda bi, j: (0, j)),
                  pl.BlockSpec((d, fc), lambda bi, j: (0, n_chunks + j)),
                  pl.BlockSpec((fc, d), lambda bi, j: (j, 0)),
                  pl.BlockSpec((1, d), lambda bi, j: (0, 0)),
                  pl.BlockSpec(memory_space=pl.ANY),
                  pl.BlockSpec(memory_space=pl.ANY)],
        out_specs=pl.BlockSpec(memory_space=pl.ANY),
        out_shape=jax.ShapeDtypeStruct((b, seq, d), F32),
        scratch_shapes=[pltpu.VMEM(*s) for s in vmem_scratch] + [
            pltpu.SemaphoreType.DMA((n_tiles,)),
            pltpu.SemaphoreType.DMA((2,)),
            pltpu.SemaphoreType.DMA((n_tiles,))],
        compiler_params=_params(("arbitrary", "arbitrary"),
                                [((d, fc), F32, 2)] * 2 + [((fc, d), F32, 2)], vmem_scratch),
        name="ffn",
    )(w_gate_up, w_gate_up, w_down, g, hn2, h1)


def kernel(x, meta_tokens, norm_mix_g, w_in, b_gate, pool_w, pool_scale, conv_w, conv_out_w, w_o,
           norm_ffn_g, w_gate_up, w_down, norm_final_g):
    b, seq, d = x.shape
    row = lambda v: v.reshape(1, -1)
    w_pool = _fold_pool(w_in, pool_w, row(pool_scale))
    gaya, z, gs, conv_out_bf, w_o_bf = _inproj(x, meta_tokens, row(norm_mix_g), w_pool, w_in, row(b_gate),
                                               conv_w, conv_out_w, w_o)
    m = b * seq
    h1, hn2 = _mix(z.reshape(m, d), gaya.reshape(m, d), gs.reshape(m, d), x.reshape(m, d),
                   conv_out_bf, w_o_bf, row(norm_ffn_g))
    return _ffn(hn2.reshape(b, seq, d), h1.reshape(b, seq, d), w_gate_up, w_down, row(norm_final_g))
```

```python
import math

import jax
import jax.numpy as jnp
from jax import lax
from jax.experimental import pallas as pl
from jax.experimental.pallas import tpu as pltpu

D_MODEL = 2048
N_META = 16
POOL_WINDOWS = (2, 4, 8, 16)
POOL_GROUP = D_MODEL // len(POOL_WINDOWS)
CONV_K = 3
FFN_HIDDEN = ((8 * D_MODEL // 3 + 255) // 256) * 256
EPS = 1e-6

F32 = jnp.float32
BF16 = jnp.bfloat16

MIB = 1024 * 1024
V7X_VMEM_BYTES = 64 * MIB
COMPILER_TEMP_BYTES = 8 * MIB
ROW_TILE = 1024
NORM_ROWS = 256
IN_CHUNK = 256
FFN_CHUNK = 512
MIX_ROWS = 512
DOWN_COLS = 512
FINISH_ROWS = 256
FOLD_COLS = 512

assert N_META >= max(POOL_WINDOWS) and N_META >= CONV_K
assert POOL_WINDOWS == (2, 4, 8, 16)


def _dot(a, b):
    return lax.dot_general(a, b, (((1,), (0,)), ((), ())), preferred_element_type=F32)


def _rms(xf, g):
    ms = jnp.mean(xf * xf, axis=-1, keepdims=True)
    return xf * lax.rsqrt(ms + EPS) * g


def _sigmoid(x):
    return 0.5 * (jnp.tanh(0.5 * x) + 1.0)


def _nbytes(shape, dtype):
    return math.prod(shape) * jnp.dtype(dtype).itemsize


def _params(semantics, windows, scratch=(), whole_vmem=False):
    declared = sum(_nbytes(s, t) * n for s, t, n in windows) + sum(_nbytes(s, t) for s, t in scratch)
    limit = V7X_VMEM_BYTES if whole_vmem else declared + COMPILER_TEMP_BYTES
    assert declared + COMPILER_TEMP_BYTES <= limit <= V7X_VMEM_BYTES, (declared, limit)
    return pltpu.CompilerParams(dimension_semantics=semantics, vmem_limit_bytes=limit)


def _fold_pool_kernel(wu_ref, pw_ref, ps_ref, o_ref):
    o_ref[...] = (_dot(wu_ref[...], pw_ref[0]) * ps_ref[...]).astype(BF16)


def _fold_pool(w_in, pool_w, pool_scale):
    d = w_in.shape[0]
    pg = POOL_GROUP
    fc = FOLD_COLS
    per_group = pg // fc
    cols = lambda g, c: (0, g * per_group + c)
    return pl.pallas_call(
        _fold_pool_kernel,
        grid=(len(POOL_WINDOWS), per_group),
        in_specs=[pl.BlockSpec((d, pg), lambda g, c: (0, g)),
                  pl.BlockSpec((1, pg, fc), lambda g, c: (g, 0, c)),
                  pl.BlockSpec((1, fc), cols)],
        out_specs=pl.BlockSpec((d, fc), cols),
        out_shape=jax.ShapeDtypeStruct((d, d), BF16),
        compiler_params=_params(("arbitrary", "arbitrary"),
                                [((d, pg), F32, 2), ((pg, fc), F32, 2), ((d, fc), BF16, 2)], whole_vmem=True),
        name="fold_pool",
    )(w_in, pool_w, pool_scale)


def _inproj_kernel(meta_ref, g_ref, wp_ref, wga_ref, wgb_ref, wgc_ref, wv_ref, wgr_ref, bga_ref, bgr_ref,
                   cw_ref, cow_ref, wo_ref, x_hbm, gaya_ref, z_ref, gs_ref, cow_bf_ref, wo_bf_ref,
                   hn_s, stage, sem_x):
    bi = pl.program_id(0)
    chunk = pl.program_id(1)
    last_b = pl.num_programs(0) - 1
    n_pieces = (hn_s.shape[1] - N_META) // NORM_ROWS
    slot = bi % 2
    g = g_ref[...]
    cow_bf_ref[...] = cow_ref[...].astype(BF16)
    wo_bf_ref[...] = wo_ref[...].astype(BF16)

    def x_copy(batch, piece, buf):
        rows = pl.ds(piece * NORM_ROWS, NORM_ROWS)
        return pltpu.make_async_copy(x_hbm.at[batch, rows, :], stage.at[buf], sem_x.at[buf])

    def norm_piece(dst_slot, piece, buf):
        rows = pl.ds(pl.multiple_of(N_META + piece * NORM_ROWS, N_META), NORM_ROWS)
        hn_s[dst_slot, rows, :] = _rms(stage[buf], g).astype(BF16)

    @pl.when((bi == 0) & (chunk == 0))
    def _():
        meta_n = _rms(meta_ref[...], g).astype(BF16)
        hn_s[0, 0:N_META, :] = meta_n
        hn_s[1, 0:N_META, :] = meta_n
        x_copy(0, 0, 0).start()
        for piece in range(n_pieces):
            if piece + 1 < n_pieces:
                x_copy(0, piece + 1, (piece + 1) % 2).start()
            x_copy(0, piece, piece % 2).wait()
            norm_piece(0, piece, piece % 2)

    @pl.when(bi < last_b)
    def _():
        x_copy(bi + 1, chunk, 0).start()

    grp = chunk // (POOL_GROUP // IN_CHUNK)
    w0 = cw_ref[0:1, :]
    w1 = cw_ref[1:2, :]
    w2 = cw_ref[2:3, :]
    inv_window = jnp.where(grp == 0, 0.5, jnp.where(grp == 1, 0.25, jnp.where(grp == 2, 0.125, 0.0625)))
    n_tiles = (hn_s.shape[1] - N_META) // ROW_TILE

    def body(i, carry):
        r = pl.multiple_of(i * ROW_TILE, ROW_TILE)
        lhs = hn_s[slot, pl.ds(r, ROW_TILE + N_META), :]
        lhs_out = lhs[N_META:, :]

        y = _dot(lhs, wp_ref[...])
        cv = _dot(lhs, wgc_ref[...]) * _dot(lhs, wv_ref[...])
        s = y + pltpu.roll(y, 1, 0)
        s = jnp.where(grp >= 1, s + pltpu.roll(s, 2, 0), s)
        s = jnp.where(grp >= 2, s + pltpu.roll(s, 4, 0), s)
        s = jnp.where(grp >= 3, s + pltpu.roll(s, 8, 0), s)
        pooled = (s * inv_window - y)[N_META:, :]

        conv = w0 * pltpu.roll(cv, 2, 0) + w1 * pltpu.roll(cv, 1, 0) + w2 * cv
        ga = _dot(lhs_out, wga_ref[...]) + bga_ref[...]
        gaya_ref[0, pl.ds(r, ROW_TILE), :] = (_sigmoid(ga) * pooled).astype(BF16)
        gr = _dot(lhs_out, wgr_ref[...]) + bgr_ref[...]
        gs_ref[0, pl.ds(r, ROW_TILE), :] = _sigmoid(gr).astype(BF16)
        z = _dot(lhs_out, wgb_ref[...]) * conv[N_META:, :]
        z_ref[0, pl.ds(r, ROW_TILE), :] = z.astype(BF16)
        return carry

    lax.fori_loop(0, n_tiles, body, 0)

    @pl.when(bi < last_b)
    def _():
        x_copy(bi + 1, chunk, 0).wait()
        norm_piece(1 - slot, chunk, 0)


def _inproj(x, meta, g, w_pool, w_in, b_gate, conv_w, conv_out_w, w_o):
    b, seq, d = x.shape
    cc = IN_CHUNK
    n_chunks = d // cc
    assert seq == n_chunks * NORM_ROWS
    slab = d // (b * n_chunks)
    slab_spec = pl.BlockSpec((slab, d), lambda bi, c: (bi * n_chunks + c, 0))
    wspec = lambda first: pl.BlockSpec((d, cc), lambda bi, c: (0, first + c))
    bspec = lambda first: pl.BlockSpec((1, cc), lambda bi, c: (0, first + c))
    out_spec = pl.BlockSpec((1, seq, cc), lambda bi, c: (bi, 0, c))
    hn_scratch = ((2, seq + N_META, d), BF16)
    x_stage = ((2, NORM_ROWS, d), F32)
    return pl.pallas_call(
        _inproj_kernel,
        grid=(b, n_chunks),
        in_specs=[pl.BlockSpec((N_META, d), lambda bi, c: (0, 0)),
                  pl.BlockSpec((1, d), lambda bi, c: (0, 0)),
                  wspec(0),
                  wspec(4 * n_chunks), wspec(1 * n_chunks), wspec(2 * n_chunks), wspec(3 * n_chunks),
                  wspec(5 * n_chunks),
                  bspec(0), bspec(n_chunks),
                  pl.BlockSpec((CONV_K, cc), lambda bi, c: (0, c)),
                  slab_spec, slab_spec,
                  pl.BlockSpec(memory_space=pl.ANY)],
        out_specs=[out_spec, out_spec, out_spec, slab_spec, slab_spec],
        out_shape=[jax.ShapeDtypeStruct((b, seq, d), BF16)] * 3 + [jax.ShapeDtypeStruct((d, d), BF16)] * 2,
        scratch_shapes=[pltpu.VMEM(*hn_scratch), pltpu.VMEM(*x_stage), pltpu.SemaphoreType.DMA((2,))],
        compiler_params=_params(
            ("arbitrary", "arbitrary"),
            [((d, cc), F32, 2)] * 5 + [((d, cc), BF16, 2), ((N_META, d), F32, 2)]
            + [((slab, d), F32, 2)] * 2 + [((slab, d), BF16, 2)] * 2 + [((seq, cc), BF16, 2)] * 3,
            [hn_scratch, x_stage]),
        name="inproj_mixers",
    )(meta, g, w_pool, w_in, w_in, w_in, w_in, w_in, b_gate, b_gate, conv_w, conv_out_w, w_o, x)


def _mix_kernel(z_ref, gaya_ref, gs_ref, x_ref, cw_ref, wo_ref, g_ref, h1_ref, hn2_ref):
    yb = _dot(z_ref[...], cw_ref[...])
    mix = gaya_ref[...].astype(F32) + gs_ref[...].astype(F32) * yb
    h1 = x_ref[...] + _dot(mix.astype(BF16), wo_ref[...])
    h1_ref[...] = h1
    hn2_ref[...] = _rms(h1, g_ref[...]).astype(BF16)


def _mix(z, gaya, gs, x2d, conv_out_w, w_o, g):
    m, d = x2d.shape
    rows = pl.BlockSpec((MIX_ROWS, d), lambda i: (i, 0))
    whole = pl.BlockSpec((d, d), lambda i: (0, 0), pipeline_mode=pl.Buffered(1))
    return pl.pallas_call(
        _mix_kernel,
        grid=(m // MIX_ROWS,),
        in_specs=[rows, rows, rows, rows, whole, whole, pl.BlockSpec((1, d), lambda i: (0, 0))],
        out_specs=[rows, rows],
        out_shape=[jax.ShapeDtypeStruct((m, d), F32), jax.ShapeDtypeStruct((m, d), BF16)],
        compiler_params=_params(
            ("arbitrary",),
            [((MIX_ROWS, d), BF16, 2)] * 3 + [((MIX_ROWS, d), F32, 2)] + [((d, d), BF16, 1)] * 2
            + [((MIX_ROWS, d), F32, 2), ((MIX_ROWS, d), BF16, 2)]),
        name="mix_out_proj",
    )(z, gaya, gs, x2d, conv_out_w, w_o, g)


def _ffn_kernel(wg_ref, wu_ref, wd_ref, g_ref, hn_hbm, h1_hbm, out_hbm,
                hn_s, acc, stage, sem_hn, sem_h1, sem_out):
    bi = pl.program_id(0)
    j = pl.program_id(1)
    last_b = pl.num_programs(0) - 1
    last_j = pl.num_programs(1) - 1
    seq, d = acc.shape
    n_tiles = seq // ROW_TILE
    n_fin = seq // FINISH_ROWS
    fin_per_tile = ROW_TILE // FINISH_ROWS

    def hn_copy(batch, t):
        rows = pl.ds(t * ROW_TILE, ROW_TILE)
        return pltpu.make_async_copy(hn_hbm.at[batch, rows, :], hn_s.at[rows, :], sem_hn.at[t])

    def h1_copy(s):
        rows = pl.ds(s * FINISH_ROWS, FINISH_ROWS)
        return pltpu.make_async_copy(h1_hbm.at[bi, rows, :], stage.at[s % 2], sem_h1.at[s % 2])

    def out_copy(batch, t):
        rows = pl.ds(t * ROW_TILE, ROW_TILE)
        return pltpu.make_async_copy(acc.at[rows, :], out_hbm.at[batch, rows, :], sem_out.at[t])

    @pl.when((bi == 0) & (j == 0))
    def _():
        for t in range(n_tiles):
            hn_copy(0, t).start()
        acc[...] = jnp.zeros_like(acc)

    @pl.when(j < n_fin)
    def _():
        h1_copy(j).start()

    for t in range(n_tiles):
        rows = pl.ds(t * ROW_TILE, ROW_TILE)

        @pl.when(j == 0)
        def _():
            hn_copy(bi, t).wait()

            @pl.when(bi > 0)
            def _():
                out_copy(bi - 1, t).wait()

        lhs = hn_s[rows, :]
        gate = _dot(lhs, wg_ref[...])
        up = _dot(lhs, wu_ref[...])
        half_gate = 0.5 * gate
        act = (half_gate * (1.0 + jnp.tanh(half_gate)) * up).astype(BF16)
        for n in range(d // DOWN_COLS):
            cols = slice(n * DOWN_COLS, (n + 1) * DOWN_COLS)
            prev = jnp.where(j == 0, 0.0, acc[rows, cols])
            acc[rows, cols] = prev + _dot(act, wd_ref[:, cols])

        @pl.when((j == last_j) & (bi < last_b))
        def _():
            hn_copy(bi + 1, t).start()

    @pl.when((j >= 1) & (j <= n_fin))
    def _():
        piece = j - 1
        h1_copy(piece).wait()
        rows = pl.ds(pl.multiple_of(piece * FINISH_ROWS, FINISH_ROWS), FINISH_ROWS)
        acc[rows, :] += stage[piece % 2]

    @pl.when(j == last_j)
    def _():
        for s in range(n_fin):
            rows = pl.ds(s * FINISH_ROWS, FINISH_ROWS)
            acc[rows, :] = _rms(acc[rows, :], g_ref[...])
            if (s + 1) % fin_per_tile == 0:
                out_copy(bi, s // fin_per_tile).start()

        @pl.when(bi == last_b)
        def _():
            for t in range(n_tiles):
                out_copy(bi, t).wait()


def _ffn(hn2, h1, w_gate_up, w_down, g):
    b, seq, d = hn2.shape
    fc = FFN_CHUNK
    n_chunks = FFN_HIDDEN // fc
    n_tiles = seq // ROW_TILE
    assert n_chunks > seq // FINISH_ROWS
    vmem_scratch = [((seq, d), BF16), ((seq, d), F32), ((2, FINISH_ROWS, d), F32)]
    return pl.pallas_call(
        _ffn_kernel,
        grid=(b, n_chunks),
        in_specs=[pl.BlockSpec((d, fc), lambda bi, j: (0, j)),
                  pl.BlockSpec((d, fc), lambda bi, j: (0, n_chunks + j)),
                  pl.BlockSpec((fc, d), lambda bi, j: (j, 0)),
                  pl.BlockSpec((1, d), lambda bi, j: (0, 0)),
                  pl.BlockSpec(memory_space=pl.ANY),
                  pl.BlockSpec(memory_space=pl.ANY)],
        out_specs=pl.BlockSpec(memory_space=pl.ANY),
        out_shape=jax.ShapeDtypeStruct((b, seq, d), F32),
        scratch_shapes=[pltpu.VMEM(*s) for s in vmem_scratch] + [
            pltpu.SemaphoreType.DMA((n_tiles,)),
            pltpu.SemaphoreType.DMA((2,)),
            pltpu.SemaphoreType.DMA((n_tiles,))],
        compiler_params=_params(("arbitrary", "arbitrary"),
                                [((d, fc), F32, 2)] * 2 + [((fc, d), F32, 2)], vmem_scratch),
        name="ffn",
    )(w_gate_up, w_gate_up, w_down, g, hn2, h1)


def kernel(x, meta_tokens, norm_mix_g, w_in, b_gate, pool_w, pool_scale, conv_w, conv_out_w, w_o,
           norm_ffn_g, w_gate_up, w_down, norm_final_g):
    b, seq, d = x.shape
    row = lambda v: v.reshape(1, -1)
    w_pool = _fold_pool(w_in, pool_w, row(pool_scale))
    gaya, z, gs, conv_out_bf, w_o_bf = _inproj(x, meta_tokens, row(norm_mix_g), w_pool, w_in, row(b_gate),
                                               conv_w, conv_out_w, w_o)
    m = b * seq
    h1, hn2 = _mix(z.reshape(m, d), gaya.reshape(m, d), gs.reshape(m, d), x.reshape(m, d),
                   conv_out_bf, w_o_bf, row(norm_ffn_g))
    return _ffn(hn2.reshape(b, seq, d), h1.reshape(b, seq, d), w_gate_up, w_down, row(norm_final_g))
```
